```python
import math
import jax, jax.numpy as jnp
from jax import lax
import numpy as np

D_MODEL = 1024
BATCH = 4
SEQ = 8192
DEPTH = 4

MEM_LEN = 256
EPS = 1e-6
NEG = -1e30
MAX_POS_OFFSET = 4096

MLA_HEADS = 4
MLA_NOPE = 64
MLA_ROPE = 32
MLA_V = 64
MLA_Q_RANK = 192
MLA_KV_RANK = 128
MLA_OUT = MLA_HEADS * MLA_V
ROPE_THETA = 10000.0

LRU_WIDTH = 512
LRU_BLOCKS = 8
LRU_BLOCK = LRU_WIDTH // LRU_BLOCKS
CONV_W = 4
LRU_C = 8.0

DIFF_HEADS = 4
DIFF_DK = 32
DIFF_DV = 2 * DIFF_DK
DIFF_QK = DIFF_HEADS * 2 * DIFF_DK
DIFF_OUT = DIFF_HEADS * DIFF_DV

MIX_WIDTH = MLA_OUT + LRU_WIDTH + DIFF_OUT

IN_SPLITS = (
    MLA_Q_RANK,
    MLA_Q_RANK + MLA_KV_RANK,
    MLA_Q_RANK + MLA_KV_RANK + MLA_ROPE,
    MLA_Q_RANK + MLA_KV_RANK + MLA_ROPE + LRU_WIDTH,
    MLA_Q_RANK + MLA_KV_RANK + MLA_ROPE + 2 * LRU_WIDTH,
    MLA_Q_RANK + MLA_KV_RANK + MLA_ROPE + 2 * LRU_WIDTH + DIFF_QK,
    MLA_Q_RANK + MLA_KV_RANK + MLA_ROPE + 2 * LRU_WIDTH + 2 * DIFF_QK,
)
IN_WIDTH = MLA_Q_RANK + MLA_KV_RANK + MLA_ROPE + 2 * LRU_WIDTH + 2 * DIFF_QK + DIFF_OUT

N_BUCKETS = 32
MAX_DIST = 128

CROSS_HEADS = 4
CROSS_DH = D_MODEL // CROSS_HEADS

PEER_HEADS = 8
PEER_NKEYS = 128
PEER_EXPERTS = PEER_NKEYS * PEER_NKEYS
PEER_TOPK = 16
PEER_DQ = 128

Q_BLOCK = 128
TOKEN_BLOCK = 128

kernel_name = 'hybrid_mla_rglru_diffattn_peer'


def rmsnorm(x, g):
    x32 = x.astype(jnp.float32)
    y = x32 * lax.rsqrt(jnp.mean(x32 * x32, axis=-1, keepdims=True) + EPS)
    return (y * g.astype(jnp.float32)).astype(x.dtype)


def rope(x, pos):
    r = x.shape[-1]
    inv = ROPE_THETA ** (-jnp.arange(0, r, 2, dtype=jnp.float32) / r)
    ang = pos.astype(jnp.float32)[..., None] * inv
    if x.ndim == 4:
        ang = ang[:, :, None, :]
    cos, sin = jnp.cos(ang), jnp.sin(ang)
    x32 = x.astype(jnp.float32)
    x1, x2 = x32[..., : r // 2], x32[..., r // 2:]
    return jnp.concatenate([x1 * cos - x2 * sin, x1 * sin + x2 * cos], axis=-1).astype(x.dtype)


def t5_bucket(rel):
    n = jnp.maximum(-rel, 0)
    max_exact = N_BUCKETS // 2
    large = max_exact + (jnp.log(jnp.maximum(n, 1).astype(jnp.float32) / max_exact)
                         / math.log(MAX_DIST / max_exact) * (N_BUCKETS - max_exact)).astype(jnp.int32)
    large = jnp.minimum(large, N_BUCKETS - 1)
    return jnp.where(n < max_exact, n, large)


def causal_depthwise_conv(x, w, b):
    s = x.shape[1]
    xp = jnp.pad(x, ((0, 0), (CONV_W - 1, 0), (0, 0)))
    y = b
    for k in range(CONV_W):
        y = y + xp[:, k:k + s] * w[k]
    return y


def _linear_combine(c1, c2):
    a1, b1 = c1
    a2, b2 = c2
    return a1 * a2, a2 * b1 + b2


def rg_lru_branch(x_in, gate_in, conv_w, conv_b, w_a, b_a, w_x, b_x, lam):
    b, s, _ = x_in.shape
    xc = causal_depthwise_conv(x_in, conv_w, conv_b)
    xb = xc.reshape(b, s, LRU_BLOCKS, LRU_BLOCK)
    r = jax.nn.sigmoid(jnp.einsum('bsgi,gij->bsgj', xb, w_a) + b_a).reshape(b, s, LRU_WIDTH)
    i = jax.nn.sigmoid(jnp.einsum('bsgi,gij->bsgj', xb, w_x) + b_x).reshape(b, s, LRU_WIDTH)
    log_a = -LRU_C * r.astype(jnp.float32) * jax.nn.softplus(-lam.astype(jnp.float32))
    a = jnp.exp(log_a)
    u = jnp.sqrt(-jnp.expm1(2.0 * log_a)) * (i * xc).astype(jnp.float32)
    _, h = lax.associative_scan(_linear_combine, (a, u), axis=1)
    return h.astype(x_in.dtype) * jax.nn.gelu(gate_in, approximate=False)


def causal_block_attention(q_nope, q_rope, k_nope, k_rope, v_mla, q_d, k_d, v_d, positions, rel_bias, lam):
    b, s = positions.shape
    n_blocks = s // Q_BLOCK
    k_idx = jnp.arange(s)
    mla_scale = 1.0 / math.sqrt(MLA_NOPE + MLA_ROPE)
    diff_scale = 1.0 / math.sqrt(DIFF_DK)

    def one_block(blk):
        start = blk * Q_BLOCK
        sl = lambda t: lax.dynamic_slice_in_dim(t, start, Q_BLOCK, axis=1)
        q_idx = start + jnp.arange(Q_BLOCK)
        mask = k_idx[None, :] <= q_idx[:, None]
        s1 = (jnp.einsum('bqhd,bkhd->bhqk', sl(q_nope), k_nope)
              + jnp.einsum('bqhr,bkr->bhqk', sl(q_rope), k_rope)).astype(jnp.float32) * mla_scale
        p1 = jax.nn.softmax(jnp.where(mask, s1, NEG), axis=-1).astype(v_mla.dtype)
        o_mla = jnp.einsum('bhqk,bkhd->bqhd', p1, v_mla)
        rel = positions[:, None, :] - sl(positions)[:, :, None]
        bias = jnp.moveaxis(rel_bias[t5_bucket(rel)], -1, 1).astype(jnp.float32)
        s2 = jnp.einsum('bqhcd,bkhcd->bhcqk', sl(q_d), k_d).astype(jnp.float32) * diff_scale + bias[:, :, None]
        p2 = jax.nn.softmax(jnp.where(mask, s2, NEG), axis=-1)
        w = (p2[:, :, 0] - lam * p2[:, :, 1]).astype(v_d.dtype)
        o_diff = jnp.einsum('bhqk,bkhd->bqhd', w, v_d)
        return o_mla, o_diff

    o_mla, o_diff = lax.map(one_block, jnp.arange(n_blocks))
    o_mla = jnp.moveaxis(o_mla, 0, 1).reshape(b, s, MLA_HEADS, MLA_V)
    o_diff = jnp.moveaxis(o_diff, 0, 1).reshape(b, s, DIFF_HEADS, DIFF_DV)
    return o_mla, o_diff


def hybrid_mixer(h, positions, rel_bias, layer_idx, w_in, q_norm, w_uq, kv_norm, w_ukv,
                 conv_w, conv_b, w_a, b_a, w_x, b_x, lru_lambda, diff_lambda, diff_norm,
                 out_norm_mla, out_norm_lru, w_out):
    b, s, _ = h.shape
    c_q, c_kv, k_r, x_lru, g_lru, q_d, k_d, v_d = jnp.split(h @ w_in, IN_SPLITS, axis=-1)
    q = (rmsnorm(c_q, q_norm) @ w_uq).reshape(b, s, MLA_HEADS, MLA_NOPE + MLA_ROPE)
    q_nope, q_rope = q[..., :MLA_NOPE], rope(q[..., MLA_NOPE:], positions)
    kv = (rmsnorm(c_kv, kv_norm) @ w_ukv).reshape(b, s, MLA_HEADS, MLA_NOPE + MLA_V)
    k_nope, v_mla = kv[..., :MLA_NOPE], kv[..., MLA_NOPE:]
    k_rope = rope(k_r, positions)
    q_d = q_d.reshape(b, s, DIFF_HEADS, 2, DIFF_DK)
    k_d = k_d.reshape(b, s, DIFF_HEADS, 2, DIFF_DK)
    v_d = v_d.reshape(b, s, DIFF_HEADS, DIFF_DV)
    lam_init = 0.8 - 0.6 * math.exp(-0.3 * layer_idx)
    dl = diff_lambda.astype(jnp.float32)
    lam = jnp.exp(jnp.sum(dl[0] * dl[1])) - jnp.exp(jnp.sum(dl[2] * dl[3])) + lam_init
    o_mla, o_diff = causal_block_attention(q_nope, q_rope, k_nope, k_rope, v_mla,
                                           q_d, k_d, v_d, positions, rel_bias, lam)
    o_diff = (rmsnorm(o_diff, diff_norm) * (1.0 - lam_init)).reshape(b, s, DIFF_OUT)
    o_mla = rmsnorm(o_mla.reshape(b, s, MLA_OUT), out_norm_mla)
    o_lru = rmsnorm(rg_lru_branch(x_lru, g_lru, conv_w, conv_b, w_a, b_a, w_x, b_x, lru_lambda), out_norm_lru)
    y = jnp.concatenate([o_mla, o_lru, o_diff], axis=-1)
    return y @ w_out


def memory_cross_attention(h, mem_n, w_cq, w_ckv, w_co):
    b, s, d = h.shape
    m = mem_n.shape[1]
    q = (h @ w_cq).reshape(b, s, CROSS_HEADS, CROSS_DH)
    k, v = jnp.split(mem_n @ w_ckv, 2, axis=-1)
    k = k.reshape(b, m, CROSS_HEADS, CROSS_DH)
    v = v.reshape(b, m, CROSS_HEADS, CROSS_DH)
    sc = jnp.einsum('bshd,bmhd->bhsm', q, k).astype(jnp.float32) * (1.0 / math.sqrt(CROSS_DH))
    p = jax.nn.softmax(sc, axis=-1).astype(v.dtype)
    o = jnp.einsum('bhsm,bmhd->bshd', p, v).reshape(b, s, d)
    return o @ w_co


def peer_ffn(h, w_q, keys, u, v):
    b, s, d = h.shape
    q = (h @ w_q).reshape(b, s, PEER_HEADS, 2, PEER_DQ // 2)
    sc = jnp.einsum('bshcd,cnd->bshcn', q, keys).astype(jnp.float32)
    top_s, top_i = lax.top_k(sc, PEER_TOPK)
    cand = top_s[..., 0, :, None] + top_s[..., 1, None, :]
    best_s, best_c = lax.top_k(cand.reshape(b, s, PEER_HEADS, PEER_TOPK * PEER_TOPK), PEER_TOPK)
    ia = best_c // PEER_TOPK
    ib = best_c % PEER_TOPK
    experts = (jnp.take_along_axis(top_i[..., 0, :], ia, axis=-1) * PEER_NKEYS
               + jnp.take_along_axis(top_i[..., 1, :], ib, axis=-1))
    g = jax.nn.softmax(best_s, axis=-1).astype(h.dtype)
    n_blk = (b * s) // TOKEN_BLOCK
    hb = h.reshape(n_blk, TOKEN_BLOCK, d)
    eb = experts.reshape(n_blk, TOKEN_BLOCK, PEER_HEADS, PEER_TOPK)
    gb = g.reshape(n_blk, TOKEN_BLOCK, PEER_HEADS, PEER_TOPK)

    def block(args):
        xt, et, gt = args
        act = jax.nn.gelu(jnp.einsum('td,thkd->thk', xt, u[et]), approximate=False)
        return jnp.einsum('thk,thkd->td', gt * act, v[et])

    y = lax.map(block, (hb, eb, gb))
    return y.reshape(b, s, d)


def setup_inputs(seed: int = 0) -> dict:
    key = jax.random.key(seed)
    ks = jax.random.split(key, 40)
    f32 = jnp.float32
    L, D = DEPTH, D_MODEL

    def nrm(i, shape, scale):
        return jax.random.normal(ks[i], shape, f32) * scale

    def gain(i, shape):
        return 1.0 + 0.02 * jax.random.normal(ks[i], shape, f32)

    x = nrm(0, (BATCH, SEQ, D), 1.0)
    mem = nrm(1, (BATCH, MEM_LEN, D), 1.0)
    offset = jax.random.randint(ks[2], (BATCH, 1), 0, MAX_POS_OFFSET, dtype=jnp.int32)
    positions = offset + jnp.arange(SEQ, dtype=jnp.int32)[None, :]
    rel_bias = nrm(3, (N_BUCKETS, DIFF_HEADS), 0.2)
    norm_mix = gain(4, (L, D))
    w_in = nrm(5, (L, D, IN_WIDTH), D ** -0.5)
    mla_q_norm = gain(6, (L, MLA_Q_RANK))
    mla_w_uq = nrm(7, (L, MLA_Q_RANK, MLA_HEADS * (MLA_NOPE + MLA_ROPE)), MLA_Q_RANK ** -0.5)
    mla_kv_norm = gain(8, (L, MLA_KV_RANK))
    mla_w_ukv = nrm(9, (L, MLA_KV_RANK, MLA_HEADS * (MLA_NOPE + MLA_V)), MLA_KV_RANK ** -0.5)
    lru_conv_w = nrm(10, (L, CONV_W, LRU_WIDTH), CONV_W ** -0.5)
    lru_conv_b = nrm(11, (L, LRU_WIDTH), 0.01)
    lru_w_a = nrm(12, (L, LRU_BLOCKS, LRU_BLOCK, LRU_BLOCK), LRU_BLOCK ** -0.5)
    lru_b_a = nrm(13, (L, LRU_BLOCKS, LRU_BLOCK), 0.01)
    lru_w_x = nrm(14, (L, LRU_BLOCKS, LRU_BLOCK, LRU_BLOCK), LRU_BLOCK ** -0.5)
    lru_b_x = nrm(15, (L, LRU_BLOCKS, LRU_BLOCK), 0.01)
    a_pow = jax.random.uniform(ks[16], (L, LRU_WIDTH), f32, 0.9, 0.999)
    a0 = a_pow ** (1.0 / LRU_C)
    lru_lambda = jnp.log(a0) - jnp.log1p(-a0)
    diff_lambda = nrm(17, (L, 4, DIFF_DK), 0.1)
    diff_norm = gain(18, (L, DIFF_DV))
    out_norm_mla = gain(19, (L, MLA_OUT))
    out_norm_lru = gain(20, (L, LRU_WIDTH))
    w_out = nrm(21, (L, MIX_WIDTH, D), MIX_WIDTH ** -0.5)
    norm_cross = gain(22, (L, D))
    norm_mem = gain(23, (L, D))
    w_cq = nrm(24, (L, D, D), D ** -0.5)
    w_ckv = nrm(25, (L, D, 2 * D), D ** -0.5)
    w_co = nrm(26, (L, D, D), D ** -0.5)
    norm_ffn = gain(27, (L, D))
    peer_w_q = nrm(28, (L, D, PEER_HEADS * PEER_DQ), D ** -0.5)
    peer_keys = nrm(29, (L, 2, PEER_NKEYS, PEER_DQ // 2), (PEER_DQ // 2) ** -0.5)
    peer_u = nrm(30, (L, PEER_EXPERTS, D), D ** -0.5)
    peer_v = nrm(31, (L, PEER_EXPERTS, D), PEER_HEADS ** -0.5)
    norm_final = gain(32, (D,))
    return {'x': x, 'mem': mem, 'positions': positions, 'rel_bias': rel_bias,
            'norm_mix': norm_mix, 'w_in': w_in,
            'mla_q_norm': mla_q_norm, 'mla_w_uq': mla_w_uq, 'mla_kv_norm': mla_kv_norm, 'mla_w_ukv': mla_w_ukv,
            'lru_conv_w': lru_conv_w, 'lru_conv_b': lru_conv_b, 'lru_w_a': lru_w_a, 'lru_b_a': lru_b_a,
            'lru_w_x': lru_w_x, 'lru_b_x': lru_b_x, 'lru_lambda': lru_lambda,
            'diff_lambda': diff_lambda, 'diff_norm': diff_norm,
            'out_norm_mla': out_norm_mla, 'out_norm_lru': out_norm_lru, 'w_out': w_out,
            'norm_cross': norm_cross, 'norm_mem': norm_mem, 'w_cq': w_cq, 'w_ckv': w_ckv, 'w_co': w_co,
            'norm_ffn': norm_ffn, 'peer_w_q': peer_w_q, 'peer_keys': peer_keys,
            'peer_u': peer_u, 'peer_v': peer_v, 'norm_final': norm_final}


def reference(x, mem, positions, rel_bias, norm_mix, w_in, mla_q_norm, mla_w_uq, mla_kv_norm, mla_w_ukv,
              lru_conv_w, lru_conv_b, lru_w_a, lru_b_a, lru_w_x, lru_b_x, lru_lambda,
              diff_lambda, diff_norm, out_norm_mla, out_norm_lru, w_out,
              norm_cross, norm_mem, w_cq, w_ckv, w_co,
              norm_ffn, peer_w_q, peer_keys, peer_u, peer_v, norm_final):
    for l in range(DEPTH):
        h = rmsnorm(x, norm_mix[l])
        x = x + hybrid_mixer(h, positions, rel_bias, l, w_in[l], mla_q_norm[l], mla_w_uq[l],
                             mla_kv_norm[l], mla_w_ukv[l], lru_conv_w[l], lru_conv_b[l],
                             lru_w_a[l], lru_b_a[l], lru_w_x[l], lru_b_x[l], lru_lambda[l],
                             diff_lambda[l], diff_norm[l], out_norm_mla[l], out_norm_lru[l], w_out[l])
        h = rmsnorm(x, norm_cross[l])
        x = x + memory_cross_attention(h, rmsnorm(mem, norm_mem[l]), w_cq[l], w_ckv[l], w_co[l])
        h = rmsnorm(x, norm_ffn[l])
        x = x + peer_ffn(h, peer_w_q[l], peer_keys[l], peer_u[l], peer_v[l])
    return rmsnorm(x, norm_final)
```

```python
import functools
import math

import numpy as np
import jax
import jax.numpy as jnp
from jax import lax
from jax.experimental import pallas as pl
from jax.experimental.pallas import tpu as pltpu

F32 = jnp.float32
BF16 = jnp.bfloat16

D_MODEL = 1024
EPS = 1e-6
NEG = -1e30

MLA_HEADS = 4
MLA_NOPE = 64
MLA_ROPE = 32
MLA_V = 64
MLA_Q_RANK = 192
MLA_KV_RANK = 128
ROPE_THETA = 10000.0

LRU_WIDTH = 512
LRU_BLOCKS = 8
CONV_W = 4
LRU_C = 8.0

DIFF_HEADS = 4
DIFF_DK = 32
DIFF_DV = 64
DIFF_MAPS = 2 * DIFF_HEADS

N_BUCKETS = 32
MAX_DIST = 128

CROSS_HEADS = 4
CROSS_DH = D_MODEL // CROSS_HEADS

PEER_HEADS = 8
PEER_NKEYS = 128
PEER_EXPERTS = PEER_NKEYS * PEER_NKEYS
PEER_TOPK = 16

LANE = 128
MLA_SCALE = 1.0 / math.sqrt(MLA_NOPE + MLA_ROPE)
DIFF_SCALE = 1.0 / math.sqrt(DIFF_DK)
CROSS_SCALE = 1.0 / math.sqrt(CROSS_DH)
INV_SQRT2 = 1.0 / math.sqrt(2.0)

_C_CQ, _C_CKV, _C_KR, _C_KRS, _C_XL, _C_GL, _C_QD, _C_KD, _C_VD, _C_END = (
    0, 256, 384, 512, 640, 1152, 1664, 2688, 2944, 3200)

VMEM_LIMIT = 56 * 1024 * 1024


def _cparams(sem):
    return pltpu.CompilerParams(dimension_semantics=sem, vmem_limit_bytes=VMEM_LIMIT)


def _rms(x, g, n=None):
    ms = jnp.sum(x * x, axis=-1, keepdims=True) * (1.0 / (n or x.shape[-1]))
    return x * lax.rsqrt(ms + EPS) * g


def _dot(a, b):
    return jnp.dot(a, b, preferred_element_type=F32)


def _dot_nt(a, b, precision=None):
    return lax.dot_general(a, b, (((1,), (1,)), ((), ())), preferred_element_type=F32,
                           precision=precision)


def _rope_kernel(pos_ref, inv_ref, c_ref, s_ref):
    ang = pos_ref[...].astype(F32) * inv_ref[...]
    lane = lax.broadcasted_iota(jnp.int32, ang.shape, 1)
    cos, sin = jnp.cos(ang), jnp.sin(ang)
    in_rope = (lane >= MLA_NOPE) & (lane < MLA_NOPE + MLA_ROPE)
    c_ref[...] = jnp.where(lane < MLA_NOPE, 1.0, jnp.where(in_rope, cos, 0.0))
    s_ref[...] = jnp.where(in_rope, jnp.where(lane < MLA_NOPE + MLA_ROPE // 2, -sin, sin), 0.0)


def _rope_tables(positions, tm):
    t = positions.size
    inv = ROPE_THETA ** (-jnp.arange(0, MLA_ROPE, 2, dtype=F32) / MLA_ROPE)
    inv_lane = jnp.concatenate([jnp.zeros((MLA_NOPE,), F32), inv, inv,
                                jnp.zeros((LANE - MLA_NOPE - MLA_ROPE,), F32)])[None, :]
    return pl.pallas_call(
        _rope_kernel,
        out_shape=(jax.ShapeDtypeStruct((t, LANE), F32),) * 2,
        grid=(t // tm,),
        in_specs=[pl.BlockSpec((tm, 1), lambda i: (i, 0)), pl.BlockSpec((1, LANE), lambda i: (0, 0))],
        out_specs=(pl.BlockSpec((tm, LANE), lambda i: (i, 0)),) * 2,
        compiler_params=_cparams(("parallel",)),
        name="rope_tables",
    )(positions.reshape(t, 1), inv_lane)


def _bias_kernel(rb_ref, o_ref, *, t):
    h = pl.program_id(0)
    mode = pl.program_id(1)
    i = lax.broadcasted_iota(jnp.int32, (t, t), 0)
    j = lax.broadcasted_iota(jnp.int32, (t, t), 1)
    rel = j - i - mode * t
    n = jnp.maximum(-rel, 0)
    max_exact = N_BUCKETS // 2
    large = max_exact + (jnp.log(jnp.maximum(n, 1).astype(F32) / max_exact)
                         / math.log(MAX_DIST / max_exact) * (N_BUCKETS - max_exact)).astype(jnp.int32)
    large = jnp.minimum(large, N_BUCKETS - 1)
    bucket = jnp.where(n < max_exact, n, large)

    def body(b, acc):
        return jnp.where(bucket == b, rb_ref[h * N_BUCKETS + b], acc)

    bias = lax.fori_loop(0, N_BUCKETS, body, jnp.zeros((t, t), F32))
    o_ref[...] = jnp.where(rel <= 0, bias, NEG)


def _bias_tiles(rel_bias, t):
    rb = jnp.concatenate([rel_bias.astype(F32).T, jnp.zeros((1, N_BUCKETS), F32)], axis=0).reshape(-1)
    return pl.pallas_call(
        functools.partial(_bias_kernel, t=t),
        out_shape=jax.ShapeDtypeStruct((DIFF_HEADS + 1, 3, t, t), F32),
        grid=(DIFF_HEADS + 1, 3),
        in_specs=[pl.BlockSpec(memory_space=pltpu.SMEM)],
        out_specs=pl.BlockSpec((None, None, t, t), lambda h, m: (h, m, 0, 0)),
        compiler_params=_cparams(("parallel", "parallel")),
        name="bias_tiles",
    )(rb)


def _memkv_kernel(mem_ref, g_ref, w_ref, kv_ref):
    h = _rms(mem_ref[...], g_ref[...]).astype(BF16)
    kv_ref[...] = _dot(h, w_ref[...]).astype(BF16)


def _mem_kv(mem2d, norm_mem, w_ckv_bf):
    depth = norm_mem.shape[0]
    rows = mem2d.shape[0]
    return pl.pallas_call(
        _memkv_kernel,
        out_shape=jax.ShapeDtypeStruct((depth, rows, 2 * D_MODEL), BF16),
        grid=(depth,),
        in_specs=[pl.BlockSpec((rows, D_MODEL), lambda l: (0, 0)),
                  pl.BlockSpec((None, 1, D_MODEL), lambda l: (l, 0, 0)),
                  pl.BlockSpec((None, D_MODEL, 2 * D_MODEL), lambda l: (l, 0, 0))],
        out_specs=pl.BlockSpec((None, rows, 2 * D_MODEL), lambda l: (l, 0, 0)),
        compiler_params=_cparams(("parallel",)),
        name="mem_kv",
    )(mem2d, norm_mem.reshape(depth, 1, D_MODEL), w_ckv_bf)


def _in_kernel(x_ref, g_ref, win_ref, qn_ref, wuq_ref, kvn_ref, wk_ref, wv_ref, c_ref, s_ref,
               qm_ref, qd_ref, km_ref, kd_ref, vm_ref, vd_ref, xl_ref, gl_ref):
    h = _rms(x_ref[...], g_ref[...]).astype(BF16)

    def proj(a, b):
        return _dot(h, win_ref[:, a:b])

    c = c_ref[...]
    sg = s_ref[...]
    cq = proj(_C_CQ, _C_CKV)
    cqn = _rms(cq, qn_ref[...], MLA_Q_RANK).astype(BF16)
    qq = _dot(cqn, wuq_ref[...])
    half = MLA_HEADS * LANE
    for hd in range(MLA_HEADS):
        blk = slice(hd * LANE, (hd + 1) * LANE)
        swp = slice(half + hd * LANE, half + (hd + 1) * LANE)
        qm_ref[:, blk] = ((qq[:, blk] * c + qq[:, swp] * sg) * MLA_SCALE).astype(BF16)
    ckvn = _rms(proj(_C_CKV, _C_KR), kvn_ref[...]).astype(BF16)
    kn = _dot(ckvn, wk_ref[...])
    kr = proj(_C_KR, _C_KRS) * c + proj(_C_KRS, _C_XL) * sg
    for hd in range(MLA_HEADS):
        blk = slice(hd * LANE, (hd + 1) * LANE)
        km_ref[:, blk] = (kn[:, blk] + kr).astype(BF16)
    vm_ref[...] = _dot(ckvn, wv_ref[...]).astype(BF16)
    xl_ref[...] = proj(_C_XL, _C_GL)
    gl_ref[...] = proj(_C_GL, _C_QD)
    qd_ref[...] = (proj(_C_QD, _C_KD) * DIFF_SCALE).astype(BF16)
    kd_ref[...] = proj(_C_KD, _C_VD).astype(BF16)
    vd_ref[...] = proj(_C_VD, _C_END).astype(BF16)


def _in_proj(x2d, g, win, qn, wuq, kvn, wk, wv, ctab, stab, tm):
    t = x2d.shape[0]
    row = lambda w: pl.BlockSpec((tm, w), lambda i: (i, 0))
    full = lambda a: pl.BlockSpec(a.shape, lambda i: (0,) * a.ndim)
    outs = [(MLA_HEADS * LANE, BF16), (DIFF_MAPS * LANE, BF16), (MLA_HEADS * LANE, BF16),
            (DIFF_MAPS * DIFF_DK, BF16), (MLA_HEADS * MLA_V, BF16), (DIFF_HEADS * DIFF_DV, BF16),
            (LRU_WIDTH, F32), (LRU_WIDTH, F32)]
    return pl.pallas_call(
        _in_kernel,
        out_shape=tuple(jax.ShapeDtypeStruct((t, w), dt) for w, dt in outs),
        grid=(t // tm,),
        in_specs=[row(D_MODEL), full(g), full(win), full(qn), full(wuq), full(kvn), full(wk), full(wv),
                  row(LANE), row(LANE)],
        out_specs=tuple(row(w) for w, _ in outs),
        compiler_params=_cparams(("parallel",)),
        name="in_proj",
    )(x2d, g, win, qn, wuq, kvn, wk, wv, ctab, stab)


def _lru_kernel(xl_ref, gl_ref, cw_ref, cb_ref, wa_ref, ba_ref, wx_ref, bx_ref, lam_ref, gn_ref,
                o_ref, xbuf, hprev, *, ts):
    @pl.when(pl.program_id(1) == 0)
    def _():
        xbuf[0:8, :] = jnp.zeros((8, LRU_WIDTH), F32)
        hprev[...] = jnp.zeros_like(hprev)

    xbuf[8:8 + ts, :] = xl_ref[...]
    xc = cb_ref[...] + jnp.zeros((ts, LRU_WIDTH), F32)
    for k in range(CONV_W):
        off = 8 - (CONV_W - 1) + k
        xc = xc + xbuf[off:off + ts, :] * cw_ref[k:k + 1, :]
    xbuf[0:8, :] = xbuf[ts:ts + 8, :]

    xcb = xc.astype(BF16)
    r = jax.nn.sigmoid(_dot(xcb, wa_ref[...]) + ba_ref[...])
    i = jax.nn.sigmoid(_dot(xcb, wx_ref[...]) + bx_ref[...])
    nl = -lam_ref[...]
    softplus = jnp.maximum(nl, 0.0) + jnp.log1p(jnp.exp(-jnp.abs(nl)))
    log_a = -LRU_C * r * softplus
    a = jnp.exp(log_a)
    th = jnp.tanh(log_a)
    one_minus_a2 = -2.0 * th / (1.0 - th)
    u = jnp.sqrt(one_minus_a2) * (i * xc)

    row = lax.broadcasted_iota(jnp.int32, (ts, LRU_WIDTH), 0)
    d = 1
    while d < ts:
        keep = row >= d
        a_sh = jnp.where(keep, pltpu.roll(a, d, 0), 1.0)
        u_sh = jnp.where(keep, pltpu.roll(u, d, 0), 0.0)
        u = a * u_sh + u
        a = a * a_sh
        d *= 2
    hs = a * hprev[0:1, :] + u
    hprev[0:1, :] = hs[ts - 1:ts, :]

    gl = gl_ref[...]
    gate = 0.5 * gl * (1.0 + lax.erf(gl * INV_SQRT2))
    o_ref[...] = _rms(hs * gate, gn_ref[...]).astype(BF16)


def _lru(xl, gl, cw, cb, wa, ba, wx, bx, lam, gn, ts):
    b, s, _ = xl.shape
    seq = pl.BlockSpec((None, ts, LRU_WIDTH), lambda bi, ci: (bi, ci, 0))
    full = lambda a: pl.BlockSpec(a.shape, lambda bi, ci: (0,) * a.ndim)
    return pl.pallas_call(
        functools.partial(_lru_kernel, ts=ts),
        out_shape=jax.ShapeDtypeStruct((b, s, LRU_WIDTH), BF16),
        grid=(b, s // ts),
        in_specs=[seq, seq, full(cw), full(cb), full(wa), full(ba), full(wx), full(bx), full(lam), full(gn)],
        out_specs=seq,
        scratch_shapes=[pltpu.VMEM((ts + 8, LRU_WIDTH), F32), pltpu.VMEM((8, LRU_WIDTH), F32)],
        compiler_params=_cparams(("parallel", "arbitrary")),
        name="rg_lru",
    )(xl, gl, cw, cb, wa, ba, wx, bx, lam, gn)


N_MAPS = MLA_HEADS + DIFF_MAPS


def _attn_kernel(qt_ref, kt_ref, mt_ref, lam0_ref, qm_ref, qd_ref, km_ref, kd_ref, vm_ref, vd_ref, bt_ref,
                 dl_ref, om_ref, od_ref, m_scr, l_scr, acc_scr):
    step = pl.program_id(1)
    kblk = kt_ref[step]
    mode = mt_ref[step]

    @pl.when(kblk == 0)
    def _():
        m_scr[...] = jnp.full(m_scr.shape, NEG, F32)
        l_scr[...] = jnp.zeros(l_scr.shape, F32)
        acc_scr[...] = jnp.zeros(acc_scr.shape, F32)

    for m in range(N_MAPS):
        if m < MLA_HEADS:
            q = qm_ref[:, m * LANE:(m + 1) * LANE]
            k = km_ref[:, m * LANE:(m + 1) * LANE]
            v = vm_ref[:, (m // 2) * LANE:(m // 2 + 1) * LANE]
            bias = bt_ref[DIFF_HEADS, mode]
        else:
            dm = m - MLA_HEADS
            hd = dm // 2
            q = qd_ref[:, dm * LANE:(dm + 1) * LANE]
            k = kd_ref[:, (dm // 4) * LANE:(dm // 4 + 1) * LANE]
            v = vd_ref[:, (hd // 2) * LANE:(hd // 2 + 1) * LANE]
            bias = bt_ref[hd, mode]
        s = _dot_nt(q, k) + bias
        m_prev = m_scr[m]
        m_new = jnp.maximum(m_prev, jnp.max(s, axis=-1, keepdims=True))
        alpha = jnp.exp(m_prev - m_new)
        p = jnp.exp(s - m_new)
        l_scr[m] = alpha * l_scr[m] + jnp.sum(p, axis=-1, keepdims=True)
        acc_scr[m] = alpha * acc_scr[m] + _dot(p.astype(BF16), v)
        m_scr[m] = m_new

    @pl.when(mode == 0)
    def _():
        dl = dl_ref[...]
        lam = (jnp.exp(jnp.sum(dl[0:1] * dl[1:2], axis=-1, keepdims=True))
               - jnp.exp(jnp.sum(dl[2:3] * dl[3:4], axis=-1, keepdims=True)) + lam0_ref[0])
        lane = lax.broadcasted_iota(jnp.int32, (acc_scr.shape[1], LANE), 1)
        low = lane < LANE // 2

        def head_out(hd, mla):
            if mla:
                return acc_scr[hd] / l_scr[hd]
            m0 = MLA_HEADS + 2 * hd
            return acc_scr[m0] / l_scr[m0] - lam * (acc_scr[m0 + 1] / l_scr[m0 + 1])

        for blk in range(2):
            sl = slice(blk * LANE, (blk + 1) * LANE)
            om_ref[:, sl] = jnp.where(low, head_out(2 * blk, True), head_out(2 * blk + 1, True))
            od_ref[:, sl] = jnp.where(low, head_out(2 * blk, False), head_out(2 * blk + 1, False))


def _attention(qm, qd, km, kd, vm, vd, bt, dl, lam0, t):
    b, s, _ = qm.shape
    nq = s // t
    qtab = np.array([q for q in range(nq) for k in range(q + 1)], np.int32)
    ktab = np.array([k for q in range(nq) for k in range(q + 1)], np.int32)
    mtab = np.minimum(qtab - ktab, 2).astype(np.int32)
    qspec = lambda w: pl.BlockSpec((None, t, w), lambda bi, si, qt, kt, mt, l0: (bi, qt[si], 0))
    kspec = lambda w: pl.BlockSpec((None, t, w), lambda bi, si, qt, kt, mt, l0: (bi, kt[si], 0))
    full = lambda a: pl.BlockSpec(a.shape, lambda bi, si, qt, kt, mt, l0: (0,) * a.ndim)
    ow = MLA_HEADS * MLA_V
    grid_spec = pltpu.PrefetchScalarGridSpec(
        num_scalar_prefetch=4,
        grid=(b, len(qtab)),
        in_specs=[qspec(qm.shape[-1]), qspec(qd.shape[-1]), kspec(km.shape[-1]), kspec(kd.shape[-1]),
                  kspec(vm.shape[-1]), kspec(vd.shape[-1]), full(bt), full(dl)],
        out_specs=(qspec(ow), qspec(ow)),
        scratch_shapes=[pltpu.VMEM((N_MAPS, t, 1), F32), pltpu.VMEM((N_MAPS, t, 1), F32),
                        pltpu.VMEM((N_MAPS, t, LANE), F32)],
    )
    return pl.pallas_call(
        _attn_kernel,
        out_shape=(jax.ShapeDtypeStruct((b, s, ow), F32),) * 2,
        grid_spec=grid_spec,
        compiler_params=_cparams(("parallel", "arbitrary")),
        name="attention",
    )(jnp.asarray(qtab), jnp.asarray(ktab), jnp.asarray(mtab), lam0, qm, qd, km, kd, vm, vd, bt, dl)


def _mid_kernel(lam1_ref, x_ref, om_ref, ol_ref, od_ref, gm_ref, gd_ref, bd_ref, wout_ref, gc_ref, wcq_ref,
                kv_ref, wco_ref, x2_ref):
    omn = _rms(om_ref[...], gm_ref[...]).astype(BF16)
    od = od_ref[...]
    ms = jnp.dot(od * od, bd_ref[...], preferred_element_type=F32, precision=lax.Precision.HIGHEST)
    odn = (od * lax.rsqrt(ms + EPS) * gd_ref[...] * lam1_ref[0]).astype(BF16)
    w0, w1 = MLA_HEADS * MLA_V, MLA_HEADS * MLA_V + LRU_WIDTH
    y = (_dot(omn, wout_ref[0:w0, :]) + _dot(ol_ref[...], wout_ref[w0:w1, :])
         + _dot(odn, wout_ref[w1:, :]))
    x1 = x_ref[...] + y
    h2 = _rms(x1, gc_ref[...]).astype(BF16)
    q = (_dot(h2, wcq_ref[...]) * CROSS_SCALE).astype(BF16)
    outs = []
    for hd in range(CROSS_HEADS):
        sl = slice(hd * CROSS_DH, (hd + 1) * CROSS_DH)
        s = _dot_nt(q[:, sl], kv_ref[:, sl])
        p = jnp.exp(s - jnp.max(s, axis=-1, keepdims=True))
        p = (p / jnp.sum(p, axis=-1, keepdims=True)).astype(BF16)
        outs.append(_dot(p, kv_ref[:, D_MODEL + hd * CROSS_DH:D_MODEL + (hd + 1) * CROSS_DH]).astype(BF16))
    o = jnp.concatenate(outs, axis=-1)
    x2_ref[...] = x1 + _dot(o, wco_ref[...])


def _mid(lam1, x2d, om, ol, od, gm, gd, bd, wout, gc, wcq, kv, wco, tm, seq):
    t = x2d.shape[0]
    mem_len = kv.shape[1]
    row = lambda w: pl.BlockSpec((tm, w), lambda i, l1: (i, 0))
    full = lambda a: pl.BlockSpec(a.shape, lambda i, l1: (0,) * a.ndim)
    grid_spec = pltpu.PrefetchScalarGridSpec(
        num_scalar_prefetch=1,
        grid=(t // tm,),
        in_specs=[row(D_MODEL), row(om.shape[-1]), row(ol.shape[-1]), row(od.shape[-1]),
                  full(gm), full(gd), full(bd), full(wout), full(gc), full(wcq),
                  pl.BlockSpec((None, mem_len, 2 * D_MODEL), lambda i, l1: ((i * tm) // seq, 0, 0)),
                  full(wco)],
        out_specs=row(D_MODEL),
    )
    return pl.pallas_call(
        _mid_kernel,
        out_shape=jax.ShapeDtypeStruct((t, D_MODEL), F32),
        grid_spec=grid_spec,
        compiler_params=_cparams(("parallel",)),
        name="out_proj_cross_attn",
    )(lam1, x2d, om, ol, od, gm, gd, bd, wout, gc, wcq, kv, wco)


N_TOP = PEER_TOPK + 1
CAND_ROWS = 88


def _top_rows(s, scr, n):
    for k in range(n):
        m = jnp.max(s, axis=0, keepdims=True)
        scr[k:k + 1, :] = m
        s = jnp.where(s >= m, NEG, s)


def _score_kernel(x_ref, g_ref, wq_ref, key_ref, h_ref, sb_ref, eb_ref, ea_ref, thr_ref, a_scr, b_scr, c_scr):
    h3 = _rms(x_ref[...], g_ref[...]).astype(BF16)
    h_ref[...] = h3
    qp = _dot(h3, wq_ref[...])
    tm = qp.shape[0]
    for hd in range(PEER_HEADS):
        qh = qp[:, hd * LANE:(hd + 1) * LANE]
        s_a = _dot_nt(key_ref[0], qh, lax.Precision.HIGHEST)
        s_b = _dot_nt(key_ref[1], qh, lax.Precision.HIGHEST)
        _top_rows(s_a, a_scr, N_TOP)
        _top_rows(s_b, b_scr, N_TOP)
        a0 = a_scr[0:1, :]
        b0 = b_scr[0:1, :]
        c_scr[0:16, :] = a0 + b_scr[0:16, :]
        for p in range(1, 8):
            c_scr[8 + 8 * p:16 + 8 * p, :] = a_scr[p:p + 1, :] + b_scr[0:8, :]
        c_scr[72:80, :] = a_scr[8:16, :] + b0
        c_scr[80:81, :] = a0 + b_scr[16:17, :]
        c_scr[81:82, :] = a_scr[16:17, :] + b0
        c_scr[82:CAND_ROWS, :] = jnp.full((CAND_ROWS - 82, tm), NEG, F32)
        cand = c_scr[...]
        c = cand
        kth = None
        for k in range(N_TOP):
            m = jnp.max(c, axis=0, keepdims=True)
            if k == PEER_TOPK - 1:
                kth = m
            c = jnp.where(c >= m, NEG, c)
        tau = 0.5 * (kth + m)
        z = jnp.sum(jnp.where(cand > tau, jnp.exp(cand - (a0 + b0)), 0.0), axis=0, keepdims=True)
        sb_ref[hd] = s_b
        eb_ref[hd] = jnp.exp(s_b - b0)
        ea_ref[hd] = jnp.exp(s_a - a0) / z
        thr_ref[hd] = tau - s_a


def _score(x2, g, wq, keyext, tm):
    t = x2.shape[0]
    row = pl.BlockSpec((tm, D_MODEL), lambda i: (i, 0))
    full = lambda a: pl.BlockSpec(a.shape, lambda i: (0,) * a.ndim)
    kt = pl.BlockSpec((PEER_HEADS, PEER_NKEYS, tm), lambda i: (0, 0, i))
    kt_shape = jax.ShapeDtypeStruct((PEER_HEADS, PEER_NKEYS, t), F32)
    return pl.pallas_call(
        _score_kernel,
        out_shape=(jax.ShapeDtypeStruct((t, D_MODEL), BF16), kt_shape, kt_shape, kt_shape, kt_shape),
        grid=(t // tm,),
        in_specs=[row, full(g), full(wq), full(keyext)],
        out_specs=(row, kt, kt, kt, kt),
        scratch_shapes=[pltpu.VMEM((24, tm), F32), pltpu.VMEM((24, tm), F32), pltpu.VMEM((CAND_ROWS, tm), F32)],
        compiler_params=_cparams(("parallel",)),
        name="peer_score",
    )(x2, g, wq, keyext)


def _peer_kernel(h_ref, x_ref, u_ref, vt_ref, sb_ref, eb_ref, ea_ref, thr_ref, o_ref, y_scr, a_scr, g_scr,
                 *, eblk, tm):
    j = pl.program_id(1)

    @pl.when(j == 0)
    def _():
        y_scr[...] = jnp.zeros(y_scr.shape, F32)

    a_scr[...] = _dot_nt(u_ref[...], h_ref[...])
    nib = eblk // PEER_NKEYS
    for ii in range(nib):
        rows = slice(ii * PEER_NKEYS, (ii + 1) * PEER_NKEYS)
        for col in range(tm // LANE):
            cs = slice(col * LANE, (col + 1) * LANE)
            w = jnp.zeros((PEER_NKEYS, LANE), F32)
            for hd in range(PEER_HEADS):
                thr = thr_ref[hd, ii:ii + 1, cs]
                ea = ea_ref[hd, ii:ii + 1, cs]
                w = w + jnp.where(sb_ref[hd, :, cs] > thr, eb_ref[hd, :, cs] * ea, 0.0)
            a = a_scr[rows, cs]
            g_scr[rows, cs] = (w * (0.5 * a * (1.0 + lax.erf(a * INV_SQRT2)))).astype(BF16)
    y_scr[...] += _dot(vt_ref[...], g_scr[...])

    @pl.when(j == pl.num_programs(1) - 1)
    def _():
        o_ref[...] = x_ref[...] + y_scr[...].T


def _peer(h3, x2, u, vt, sb, eb, ea, thr, tm, eblk):
    t = x2.shape[0]
    row = pl.BlockSpec((tm, D_MODEL), lambda i, j: (i, 0))
    kt = pl.BlockSpec((PEER_HEADS, PEER_NKEYS, tm), lambda i, j: (0, 0, i))
    it = pl.BlockSpec((PEER_HEADS, eblk // PEER_NKEYS, tm), lambda i, j: (0, j, i))
    return pl.pallas_call(
        functools.partial(_peer_kernel, eblk=eblk, tm=tm),
        out_shape=jax.ShapeDtypeStruct((t, D_MODEL), F32),
        grid=(t // tm, PEER_EXPERTS // eblk),
        in_specs=[row, row,
                  pl.BlockSpec((eblk, D_MODEL), lambda i, j: (j, 0)),
                  pl.BlockSpec((D_MODEL, eblk), lambda i, j: (0, j)),
                  kt, kt, it, it],
        out_specs=row,
        scratch_shapes=[pltpu.VMEM((D_MODEL, tm), F32), pltpu.VMEM((eblk, tm), F32),
                        pltpu.VMEM((eblk, tm), BF16)],
        compiler_params=_cparams(("parallel", "arbitrary")),
        name="peer_mix",
    )(h3, x2, u, vt, sb, eb, ea, thr)


def _final_kernel(x_ref, g_ref, o_ref):
    o_ref[...] = _rms(x_ref[...], g_ref[...])


def _final_norm(x2d, g, tm):
    t = x2d.shape[0]
    row = pl.BlockSpec((tm, D_MODEL), lambda i: (i, 0))
    return pl.pallas_call(
        _final_kernel,
        out_shape=jax.ShapeDtypeStruct((t, D_MODEL), F32),
        grid=(t // tm,),
        in_specs=[row, pl.BlockSpec((1, D_MODEL), lambda i: (0, 0))],
        out_specs=row,
        compiler_params=_cparams(("parallel",)),
        name="final_norm",
    )(x2d, g)


def _prep_weights(p):
    depth = p['w_in'].shape[0]
    w_in = p['w_in']
    z = lambda *s: jnp.zeros((depth,) + s, F32)
    c_q = w_in[..., 0:192]
    c_kv = w_in[..., 192:320]
    kr = w_in[..., 320:352]
    half = MLA_ROPE // 2
    kr_sw = jnp.concatenate([kr[..., half:], kr[..., :half]], axis=-1)
    pad_r = LANE - MLA_NOPE - MLA_ROPE
    place = lambda a: jnp.concatenate([z(D_MODEL, MLA_NOPE), a, z(D_MODEL, pad_r)], axis=-1)
    dq = w_in[..., 1376:1632].reshape(depth, D_MODEL, 2, 4, DIFF_DK)
    qd_exp = jnp.einsum('ldbjk,jm->ldbjmk', dq, jnp.eye(4, dtype=F32)).reshape(depth, D_MODEL, DIFF_MAPS * LANE)
    win = jnp.concatenate([c_q, z(D_MODEL, 256 - MLA_Q_RANK), c_kv, place(kr), place(kr_sw),
                           w_in[..., 352:864], w_in[..., 864:1376], qd_exp,
                           w_in[..., 1632:1888], w_in[..., 1888:2144]], axis=-1).astype(BF16)

    wuq = p['mla_w_uq'].reshape(depth, MLA_Q_RANK, MLA_HEADS, MLA_NOPE + MLA_ROPE)
    nope, rp = wuq[..., :MLA_NOPE], wuq[..., MLA_NOPE:]
    rp_sw = jnp.concatenate([rp[..., half:], rp[..., :half]], axis=-1)
    zq = lambda w: jnp.zeros((depth, MLA_Q_RANK, MLA_HEADS, w), F32)
    main = jnp.concatenate([nope, rp, zq(pad_r)], axis=-1).reshape(depth, MLA_Q_RANK, MLA_HEADS * LANE)
    swp = jnp.concatenate([zq(MLA_NOPE), rp_sw, zq(pad_r)], axis=-1).reshape(depth, MLA_Q_RANK, MLA_HEADS * LANE)
    wuq_ext = jnp.concatenate([main, swp], axis=-1)
    wuq_ext = jnp.concatenate([wuq_ext, jnp.zeros((depth, 256 - MLA_Q_RANK, wuq_ext.shape[-1]), F32)],
                              axis=1).astype(BF16)
    qn = jnp.concatenate([p['mla_q_norm'], jnp.zeros((depth, 256 - MLA_Q_RANK), F32)], axis=-1)[:, None, :]

    wukv = p['mla_w_ukv'].reshape(depth, MLA_KV_RANK, MLA_HEADS, MLA_NOPE + MLA_V)
    wk = jnp.concatenate([wukv[..., :MLA_NOPE], jnp.zeros((depth, MLA_KV_RANK, MLA_HEADS, LANE - MLA_NOPE), F32)],
                         axis=-1).reshape(depth, MLA_KV_RANK, MLA_HEADS * LANE).astype(BF16)
    wv = wukv[..., MLA_NOPE:].reshape(depth, MLA_KV_RANK, MLA_HEADS * MLA_V).astype(BF16)

    eye_b = jnp.eye(LRU_BLOCKS, dtype=F32)
    bdiag = lambda w: jnp.einsum('lgij,gh->lgihj', w, eye_b).reshape(depth, LRU_WIDTH, LRU_WIDTH).astype(BF16)
    row = lambda a: a.reshape(depth, 1, -1)

    grp = np.arange(DIFF_HEADS * DIFF_DV) // DIFF_DV
    bd = jnp.asarray((grp[:, None] == grp[None, :]).astype(np.float32) / DIFF_DV)

    keys = p['peer_keys']
    kz = jnp.zeros_like(keys)
    keyext = jnp.stack([jnp.concatenate([keys[:, 0], kz[:, 0]], axis=-1),
                        jnp.concatenate([kz[:, 1], keys[:, 1]], axis=-1)], axis=1)

    lam_init = np.array([0.8 - 0.6 * math.exp(-0.3 * l) for l in range(depth)], np.float32)
    return dict(
        g_mix=row(p['norm_mix']), win=win, qn=qn, wuq=wuq_ext, kvn=row(p['mla_kv_norm']), wk=wk, wv=wv,
        cw=p['lru_conv_w'], cb=row(p['lru_conv_b']), wa=bdiag(p['lru_w_a']), ba=row(p['lru_b_a']),
        wx=bdiag(p['lru_w_x']), bx=row(p['lru_b_x']), lam=row(p['lru_lambda']), gn_lru=row(p['out_norm_lru']),
        dl=p['diff_lambda'], lam0=jnp.asarray(lam_init)[:, None], lam1=jnp.asarray(1.0 - lam_init)[:, None],
        gm=row(p['out_norm_mla']), gd=row(jnp.tile(p['diff_norm'], (1, DIFF_HEADS))),
        bd=jnp.broadcast_to(bd, (depth,) + bd.shape),
        wout=p['w_out'].astype(BF16), gc=row(p['norm_cross']), wcq=p['w_cq'].astype(BF16),
        wco=p['w_co'].astype(BF16), g_ffn=row(p['norm_ffn']), wq=p['peer_w_q'].astype(BF16), keyext=keyext,
        u=p['peer_u'].astype(BF16), vt=jnp.swapaxes(p['peer_v'], 1, 2).astype(BF16),
    )


def _forward(p, *, tm_in, ts_lru, t_attn, tm_mid, tm_score, tm_peer, eblk):
    x, mem, positions = p['x'], p['mem'], p['positions']
    b, s, d = x.shape
    t = b * s
    w = _prep_weights(p)
    ctab, stab = _rope_tables(positions, min(512, t))
    bt = _bias_tiles(p['rel_bias'], t_attn)
    kv_all = _mem_kv(mem.reshape(-1, d), p['norm_mem'], p['w_ckv'].astype(BF16))
    kv_all = kv_all.reshape(kv_all.shape[0], b, mem.shape[1], 2 * d)
    w['kv'] = kv_all

    def layer(x2d, lw):
        qm, qd, km, kd, vm, vd, xl, gl = _in_proj(x2d, lw['g_mix'], lw['win'], lw['qn'], lw['wuq'], lw['kvn'],
                                                  lw['wk'], lw['wv'], ctab, stab, tm_in)
        r3 = lambda a: a.reshape(b, s, a.shape[-1])
        ol = _lru(r3(xl), r3(gl), lw['cw'], lw['cb'], lw['wa'], lw['ba'], lw['wx'], lw['bx'], lw['lam'],
                  lw['gn_lru'], ts_lru)
        om, od = _attention(r3(qm), r3(qd), r3(km), r3(kd), r3(vm), r3(vd), bt, lw['dl'], lw['lam0'], t_attn)
        f2 = lambda a: a.reshape(t, a.shape[-1])
        x2 = _mid(lw['lam1'], x2d, f2(om), f2(ol), f2(od), lw['gm'], lw['gd'], lw['bd'], lw['wout'], lw['gc'],
                  lw['wcq'], lw['kv'], lw['wco'], tm_mid, s)
        h3, sb, eb, ea, thr = _score(x2, lw['g_ffn'], lw['wq'], lw['keyext'], tm_score)
        x3 = _peer(h3, x2, lw['u'], lw['vt'], sb, eb, ea, thr, tm_peer, eblk)
        return x3, None

    x2d, _ = lax.scan(layer, x.reshape(t, d), w)
    return _final_norm(x2d, p['norm_final'][None, :], min(512, t)).reshape(b, s, d)


def kernel(x, mem, positions, rel_bias, norm_mix, w_in, mla_q_norm, mla_w_uq, mla_kv_norm, mla_w_ukv, lru_conv_w, lru_conv_b, lru_w_a, lru_b_a, lru_w_x, lru_b_x, lru_lambda, diff_lambda, diff_norm, out_norm_mla, out_norm_lru, w_out, norm_cross, norm_mem, w_cq, w_ckv, w_co, norm_ffn, peer_w_q, peer_keys, peer_u, peer_v, norm_final):
    p = dict(x=x, mem=mem, positions=positions, rel_bias=rel_bias, norm_mix=norm_mix, w_in=w_in,
             mla_q_norm=mla_q_norm, mla_w_uq=mla_w_uq, mla_kv_norm=mla_kv_norm, mla_w_ukv=mla_w_ukv,
             lru_conv_w=lru_conv_w, lru_conv_b=lru_conv_b, lru_w_a=lru_w_a, lru_b_a=lru_b_a, lru_w_x=lru_w_x,
             lru_b_x=lru_b_x, lru_lambda=lru_lambda, diff_lambda=diff_lambda, diff_norm=diff_norm,
             out_norm_mla=out_norm_mla, out_norm_lru=out_norm_lru, w_out=w_out, norm_cross=norm_cross,
             norm_mem=norm_mem, w_cq=w_cq, w_ckv=w_ckv, w_co=w_co, norm_ffn=norm_ffn, peer_w_q=peer_w_q,
             peer_keys=peer_keys, peer_u=peer_u, peer_v=peer_v, norm_final=norm_final)
    return _forward(p, tm_in=512, ts_lru=256, t_attn=512, tm_mid=512, tm_score=256, tm_peer=512, eblk=1024)
```

```python
import functools
import math

import numpy as np
import jax
import jax.numpy as jnp
from jax import lax
from jax.experimental import pallas as pl
from jax.experimental.pallas import tpu as pltpu

F32 = jnp.float32
BF16 = jnp.bfloat16

D_MODEL = 1024
EPS = 1e-6
NEG = -1e30

MLA_HEADS = 4
MLA_NOPE = 64
MLA_ROPE = 32
MLA_V = 64
MLA_Q_RANK = 192
MLA_KV_RANK = 128
ROPE_THETA = 10000.0

LRU_WIDTH = 512
LRU_BLOCKS = 8
CONV_W = 4
LRU_C = 8.0

DIFF_HEADS = 4
DIFF_DK = 32
DIFF_DV = 64
DIFF_MAPS = 2 * DIFF_HEADS

N_BUCKETS = 32
MAX_DIST = 128

CROSS_HEADS = 4
CROSS_DH = D_MODEL // CROSS_HEADS

PEER_HEADS = 8
PEER_NKEYS = 128
PEER_EXPERTS = PEER_NKEYS * PEER_NKEYS
PEER_TOPK = 16

LANE = 128
MLA_SCALE = 1.0 / math.sqrt(MLA_NOPE + MLA_ROPE)
DIFF_SCALE = 1.0 / math.sqrt(DIFF_DK)
CROSS_SCALE = 1.0 / math.sqrt(CROSS_DH)
INV_SQRT2 = 1.0 / math.sqrt(2.0)
LOG2E = math.log2(math.e)

_C_CQ, _C_CKV, _C_KR, _C_KRS, _C_XL, _C_GL, _C_QD, _C_KD, _C_VD = (
    0, 256, 384, 512, 640, 1152, 1664, 2688, 2944)

VMEM_LIMIT = 56 * 1024 * 1024


def _cparams(sem):
    return pltpu.CompilerParams(dimension_semantics=sem, vmem_limit_bytes=VMEM_LIMIT)


def _rms(x, g, n=None):
    ms = jnp.sum(x * x, axis=-1, keepdims=True) * (1.0 / (n or x.shape[-1]))
    return x * lax.rsqrt(ms + EPS) * g


def _dot(a, b):
    return jnp.dot(a, b, preferred_element_type=F32)


def _dot_nt(a, b, precision=None):
    return lax.dot_general(a, b, (((1,), (1,)), ((), ())), preferred_element_type=F32,
                           precision=precision)


def _rope_kernel(pos_ref, inv_ref, c_ref, s_ref):
    ang = pos_ref[...].astype(F32) * inv_ref[...]
    lane = lax.broadcasted_iota(jnp.int32, ang.shape, 1)
    cos, sin = jnp.cos(ang), jnp.sin(ang)
    in_rope = (lane >= MLA_NOPE) & (lane < MLA_NOPE + MLA_ROPE)
    c_ref[...] = jnp.where(lane < MLA_NOPE, 1.0, jnp.where(in_rope, cos, 0.0))
    s_ref[...] = jnp.where(in_rope, jnp.where(lane < MLA_NOPE + MLA_ROPE // 2, -sin, sin), 0.0)


def _rope_tables(positions, tm):
    t = positions.size
    inv = ROPE_THETA ** (-jnp.arange(0, MLA_ROPE, 2, dtype=F32) / MLA_ROPE)
    inv_lane = jnp.concatenate([jnp.zeros((MLA_NOPE,), F32), inv, inv,
                                jnp.zeros((LANE - MLA_NOPE - MLA_ROPE,), F32)])[None, :]
    return pl.pallas_call(
        _rope_kernel,
        out_shape=(jax.ShapeDtypeStruct((t, LANE), F32),) * 2,
        grid=(t // tm,),
        in_specs=[pl.BlockSpec((tm, 1), lambda i: (i, 0)), pl.BlockSpec((1, LANE), lambda i: (0, 0))],
        out_specs=(pl.BlockSpec((tm, LANE), lambda i: (i, 0)),) * 2,
        compiler_params=_cparams(("parallel",)),
        name="rope_tables",
    )(positions.reshape(t, 1), inv_lane)


def _bias_kernel(rb_ref, o_ref, *, t):
    h = pl.program_id(0)
    mode = pl.program_id(1)
    kk = lax.broadcasted_iota(jnp.int32, (t, t), 0)
    qq = lax.broadcasted_iota(jnp.int32, (t, t), 1)
    rel = kk - qq - mode * t
    n = jnp.maximum(-rel, 0)
    max_exact = N_BUCKETS // 2
    large = max_exact + (jnp.log(jnp.maximum(n, 1).astype(F32) / max_exact)
                         / math.log(MAX_DIST / max_exact) * (N_BUCKETS - max_exact)).astype(jnp.int32)
    large = jnp.minimum(large, N_BUCKETS - 1)
    bucket = jnp.where(n < max_exact, n, large)

    def body(b, acc):
        return jnp.where(bucket == b, rb_ref[h * N_BUCKETS + b], acc)

    bias = lax.fori_loop(0, N_BUCKETS, body, jnp.zeros((t, t), F32))
    o_ref[...] = jnp.where(rel <= 0, bias * LOG2E, NEG)


def _bias_tiles(rel_bias, t):
    rb = jnp.concatenate([rel_bias.astype(F32).T, jnp.zeros((1, N_BUCKETS), F32)], axis=0).reshape(-1)
    return pl.pallas_call(
        functools.partial(_bias_kernel, t=t),
        out_shape=jax.ShapeDtypeStruct((DIFF_HEADS + 1, 3, t, t), F32),
        grid=(DIFF_HEADS + 1, 3),
        in_specs=[pl.BlockSpec(memory_space=pltpu.SMEM)],
        out_specs=pl.BlockSpec((None, None, t, t), lambda h, m: (h, m, 0, 0)),
        compiler_params=_cparams(("parallel", "parallel")),
        name="bias_tiles",
    )(rb)


def _memkv_kernel(mem_ref, g_ref, w_ref, kv_ref):
    h = _rms(mem_ref[...], g_ref[...]).astype(BF16)
    kv_ref[...] = _dot(h, w_ref[...]).astype(BF16)


def _mem_kv(mem2d, norm_mem, w_ckv_bf):
    depth = norm_mem.shape[0]
    rows = mem2d.shape[0]
    return pl.pallas_call(
        _memkv_kernel,
        out_shape=jax.ShapeDtypeStruct((depth, rows, 2 * D_MODEL), BF16),
        grid=(depth,),
        in_specs=[pl.BlockSpec((rows, D_MODEL), lambda l: (0, 0)),
                  pl.BlockSpec((None, 1, D_MODEL), lambda l: (l, 0, 0)),
                  pl.BlockSpec((None, D_MODEL, 2 * D_MODEL), lambda l: (l, 0, 0))],
        out_specs=pl.BlockSpec((None, rows, 2 * D_MODEL), lambda l: (l, 0, 0)),
        compiler_params=_cparams(("parallel",)),
        name="mem_kv",
    )(mem2d, norm_mem.reshape(depth, 1, D_MODEL), w_ckv_bf)


def _in_kernel(x_ref, g_ref, win_ref, wvdt_ref, qn_ref, wuq_ref, kvn_ref, wk_ref, wvt_ref, c_ref, s_ref,
               qm_ref, qd_ref, km_ref, kd_ref, vmt_ref, vdt_ref, xl_ref, gl_ref):
    h = _rms(x_ref[...], g_ref[...]).astype(BF16)

    def proj(a, b):
        return _dot(h, win_ref[:, a:b])

    c = c_ref[...]
    sg = s_ref[...]
    cq = proj(_C_CQ, _C_CKV)
    cqn = _rms(cq, qn_ref[...], MLA_Q_RANK).astype(BF16)
    qq = _dot(cqn, wuq_ref[...])
    half = MLA_HEADS * LANE
    for hd in range(MLA_HEADS):
        blk = slice(hd * LANE, (hd + 1) * LANE)
        swp = slice(half + hd * LANE, half + (hd + 1) * LANE)
        qm_ref[:, blk] = ((qq[:, blk] * c + qq[:, swp] * sg) * (MLA_SCALE * LOG2E)).astype(BF16)
    ckvn = _rms(proj(_C_CKV, _C_KR), kvn_ref[...]).astype(BF16)
    kn = _dot(ckvn, wk_ref[...])
    kr = proj(_C_KR, _C_KRS) * c + proj(_C_KRS, _C_XL) * sg
    for hd in range(MLA_HEADS):
        blk = slice(hd * LANE, (hd + 1) * LANE)
        km_ref[:, blk] = (kn[:, blk] + kr).astype(BF16)
    vmt_ref[...] = _dot_nt(wvt_ref[...], ckvn).astype(BF16)
    vdt_ref[...] = _dot_nt(wvdt_ref[...], h).astype(BF16)
    xl_ref[...] = proj(_C_XL, _C_GL)
    gl_ref[...] = proj(_C_GL, _C_QD)
    qd_ref[...] = (proj(_C_QD, _C_KD) * (DIFF_SCALE * LOG2E)).astype(BF16)
    kd_ref[...] = proj(_C_KD, _C_VD).astype(BF16)


def _in_proj(x2d, g, win, wvdt, qn, wuq, kvn, wk, wvt, ctab, stab, tm):
    t = x2d.shape[0]
    row = lambda w: pl.BlockSpec((tm, w), lambda i: (i, 0))
    col = lambda w: pl.BlockSpec((w, tm), lambda i: (0, i))
    full = lambda a: pl.BlockSpec(a.shape, lambda i: (0,) * a.ndim)
    vw = MLA_HEADS * MLA_V
    shapes = [((t, MLA_HEADS * LANE), BF16, row), ((t, DIFF_MAPS * LANE), BF16, row),
              ((t, MLA_HEADS * LANE), BF16, row), ((t, DIFF_MAPS * DIFF_DK), BF16, row),
              ((vw, t), BF16, col), ((vw, t), BF16, col), ((t, LRU_WIDTH), F32, row), ((t, LRU_WIDTH), F32, row)]
    return pl.pallas_call(
        _in_kernel,
        out_shape=tuple(jax.ShapeDtypeStruct(sh, dt) for sh, dt, _ in shapes),
        grid=(t // tm,),
        in_specs=[row(D_MODEL), full(g), full(win), full(wvdt), full(qn), full(wuq), full(kvn), full(wk),
                  full(wvt), row(LANE), row(LANE)],
        out_specs=tuple(mk(sh[1] if mk is row else sh[0]) for sh, _, mk in shapes),
        compiler_params=_cparams(("parallel",)),
        name="in_proj",
    )(x2d, g, win, wvdt, qn, wuq, kvn, wk, wvt, ctab, stab)


def _lru_kernel(xl_ref, gl_ref, cw_ref, cb_ref, wa_ref, ba_ref, wx_ref, bx_ref, lam_ref, gn_ref,
                o_ref, xbuf, hprev, *, ts):
    @pl.when(pl.program_id(1) == 0)
    def _():
        xbuf[0:8, :] = jnp.zeros((8, LRU_WIDTH), F32)
        hprev[...] = jnp.zeros_like(hprev)

    xbuf[8:8 + ts, :] = xl_ref[...]
    xc = cb_ref[...] + jnp.zeros((ts, LRU_WIDTH), F32)
    for k in range(CONV_W):
        off = 8 - (CONV_W - 1) + k
        xc = xc + xbuf[off:off + ts, :] * cw_ref[k:k + 1, :]
    xbuf[0:8, :] = xbuf[ts:ts + 8, :]

    xcb = xc.astype(BF16)
    r = jax.nn.sigmoid(_dot(xcb, wa_ref[...]) + ba_ref[...])
    i = jax.nn.sigmoid(_dot(xcb, wx_ref[...]) + bx_ref[...])
    nl = -lam_ref[...]
    softplus = jnp.maximum(nl, 0.0) + jnp.log1p(jnp.exp(-jnp.abs(nl)))
    log_a = -LRU_C * r * softplus
    a = jnp.exp(log_a)
    th = jnp.tanh(log_a)
    one_minus_a2 = -2.0 * th / (1.0 - th)
    u = jnp.sqrt(one_minus_a2) * (i * xc)

    row = lax.broadcasted_iota(jnp.int32, (ts, LRU_WIDTH), 0)
    d = 1
    while d < ts:
        keep = row >= d
        a_sh = jnp.where(keep, pltpu.roll(a, d, 0), 1.0)
        u_sh = jnp.where(keep, pltpu.roll(u, d, 0), 0.0)
        u = a * u_sh + u
        a = a * a_sh
        d *= 2
    hs = a * hprev[0:1, :] + u
    hprev[0:1, :] = hs[ts - 1:ts, :]

    gl = gl_ref[...]
    gate = 0.5 * gl * (1.0 + lax.erf(gl * INV_SQRT2))
    o_ref[...] = _rms(hs * gate, gn_ref[...]).astype(BF16)


def _lru(xl, gl, cw, cb, wa, ba, wx, bx, lam, gn, ts):
    b, s, _ = xl.shape
    seq = pl.BlockSpec((None, ts, LRU_WIDTH), lambda bi, ci: (bi, ci, 0))
    full = lambda a: pl.BlockSpec(a.shape, lambda bi, ci: (0,) * a.ndim)
    return pl.pallas_call(
        functools.partial(_lru_kernel, ts=ts),
        out_shape=jax.ShapeDtypeStruct((b, s, LRU_WIDTH), BF16),
        grid=(b, s // ts),
        in_specs=[seq, seq, full(cw), full(cb), full(wa), full(ba), full(wx), full(bx), full(lam), full(gn)],
        out_specs=seq,
        scratch_shapes=[pltpu.VMEM((ts + 8, LRU_WIDTH), F32), pltpu.VMEM((8, LRU_WIDTH), F32)],
        compiler_params=_cparams(("parallel", "arbitrary")),
        name="rg_lru",
    )(xl, gl, cw, cb, wa, ba, wx, bx, lam, gn)


N_MAPS = MLA_HEADS + DIFF_MAPS


def _attn_kernel(qt_ref, kt_ref, mt_ref, lam0_ref, qm_ref, qd_ref, km_ref, kd_ref, vmt_ref, vdt_ref, bt_ref,
                 dl_ref, om_ref, od_ref, m_scr, l_scr, acc_scr):
    step = pl.program_id(1)
    kblk = kt_ref[step]
    mode = mt_ref[step]

    @pl.when(kblk == 0)
    def _():
        m_scr[...] = jnp.full(m_scr.shape, NEG, F32)
        l_scr[...] = jnp.zeros(l_scr.shape, F32)
        acc_scr[...] = jnp.zeros(acc_scr.shape, F32)

    for m in range(N_MAPS):
        row = slice(m, m + 1)
        if m < MLA_HEADS:
            q = qm_ref[:, m * LANE:(m + 1) * LANE]
            k = km_ref[:, m * LANE:(m + 1) * LANE]
            vt = vmt_ref[(m // 2) * LANE:(m // 2 + 1) * LANE, :]
            bias = bt_ref[DIFF_HEADS, mode]
        else:
            dm = m - MLA_HEADS
            hd = dm // 2
            q = qd_ref[:, dm * LANE:(dm + 1) * LANE]
            k = kd_ref[:, (dm // 4) * LANE:(dm // 4 + 1) * LANE]
            vt = vdt_ref[(hd // 2) * LANE:(hd // 2 + 1) * LANE, :]
            bias = bt_ref[hd, mode]
        s = _dot_nt(k, q) + bias
        m_prev = m_scr[row, :]
        m_new = jnp.maximum(m_prev, jnp.max(s, axis=0, keepdims=True))
        alpha = jnp.exp2(m_prev - m_new)
        p = jnp.exp2(s - m_new)
        l_scr[row, :] = alpha * l_scr[row, :] + jnp.sum(p, axis=0, keepdims=True)
        acc_scr[m] = alpha * acc_scr[m] + _dot(vt, p.astype(BF16))
        m_scr[row, :] = m_new

    @pl.when(mode == 0)
    def _():
        dl = dl_ref[...]
        lam = (jnp.exp(jnp.sum(dl[0:1] * dl[1:2], axis=-1, keepdims=True))
               - jnp.exp(jnp.sum(dl[2:3] * dl[3:4], axis=-1, keepdims=True)) + lam0_ref[0])

        def head_rows(m, hd):
            half = slice((hd % 2) * MLA_V, (hd % 2 + 1) * MLA_V)
            return acc_scr[m, half, :] / l_scr[m:m + 1, :]

        o_mla = [head_rows(hd, hd) for hd in range(MLA_HEADS)]
        o_diff = [head_rows(MLA_HEADS + 2 * hd, hd) - lam * head_rows(MLA_HEADS + 2 * hd + 1, hd)
                  for hd in range(DIFF_HEADS)]
        om_ref[...] = jnp.concatenate(o_mla, axis=0).T
        od_ref[...] = jnp.concatenate(o_diff, axis=0).T


def _attention(qm, qd, km, kd, vmt, vdt, bt, dl, lam0, t):
    b, s, _ = qm.shape
    nq = s // t
    qtab = np.array([q for q in range(nq) for k in range(q + 1)], np.int32)
    ktab = np.array([k for q in range(nq) for k in range(q + 1)], np.int32)
    mtab = np.minimum(qtab - ktab, 2).astype(np.int32)
    qspec = lambda w: pl.BlockSpec((None, t, w), lambda bi, si, qt, kt, mt, l0: (bi, qt[si], 0))
    kspec = lambda w: pl.BlockSpec((None, t, w), lambda bi, si, qt, kt, mt, l0: (bi, kt[si], 0))
    vspec = lambda w: pl.BlockSpec((w, t), lambda bi, si, qt, kt, mt, l0: (0, bi * nq + kt[si]))
    full = lambda a: pl.BlockSpec(a.shape, lambda bi, si, qt, kt, mt, l0: (0,) * a.ndim)
    ow = MLA_HEADS * MLA_V
    stat_rows = 16
    grid_spec = pltpu.PrefetchScalarGridSpec(
        num_scalar_prefetch=4,
        grid=(b, len(qtab)),
        in_specs=[qspec(qm.shape[-1]), qspec(qd.shape[-1]), kspec(km.shape[-1]), kspec(kd.shape[-1]),
                  vspec(vmt.shape[0]), vspec(vdt.shape[0]), full(bt), full(dl)],
        out_specs=(qspec(ow), qspec(ow)),
        scratch_shapes=[pltpu.VMEM((stat_rows, t), F32), pltpu.VMEM((stat_rows, t), F32),
                        pltpu.VMEM((N_MAPS, LANE, t), F32)],
    )
    return pl.pallas_call(
        _attn_kernel,
        out_shape=(jax.ShapeDtypeStruct((b, s, ow), F32),) * 2,
        grid_spec=grid_spec,
        compiler_params=_cparams(("parallel", "arbitrary")),
        name="attention",
    )(jnp.asarray(qtab), jnp.asarray(ktab), jnp.asarray(mtab), lam0, qm, qd, km, kd, vmt, vdt, bt, dl)


def _mid_kernel(lam1_ref, x_ref, om_ref, ol_ref, od_ref, gm_ref, gd_ref, bd_ref, wout_ref, gc_ref, wcq_ref,
                kv_ref, wco_ref, x2_ref):
    omn = _rms(om_ref[...], gm_ref[...]).astype(BF16)
    od = od_ref[...]
    ms = jnp.dot(od * od, bd_ref[...], preferred_element_type=F32, precision=lax.Precision.HIGHEST)
    odn = (od * lax.rsqrt(ms + EPS) * gd_ref[...] * lam1_ref[0]).astype(BF16)
    w0, w1 = MLA_HEADS * MLA_V, MLA_HEADS * MLA_V + LRU_WIDTH
    y = (_dot(omn, wout_ref[0:w0, :]) + _dot(ol_ref[...], wout_ref[w0:w1, :])
         + _dot(odn, wout_ref[w1:, :]))
    x1 = x_ref[...] + y
    h2 = _rms(x1, gc_ref[...]).astype(BF16)
    q = (_dot(h2, wcq_ref[...]) * CROSS_SCALE).astype(BF16)
    outs = []
    for hd in range(CROSS_HEADS):
        sl = slice(hd * CROSS_DH, (hd + 1) * CROSS_DH)
        s = _dot_nt(q[:, sl], kv_ref[:, sl])
        p = jnp.exp(s - jnp.max(s, axis=-1, keepdims=True))
        p = (p / jnp.sum(p, axis=-1, keepdims=True)).astype(BF16)
        outs.append(_dot(p, kv_ref[:, D_MODEL + hd * CROSS_DH:D_MODEL + (hd + 1) * CROSS_DH]).astype(BF16))
    o = jnp.concatenate(outs, axis=-1)
    x2_ref[...] = x1 + _dot(o, wco_ref[...])


def _mid(lam1, x2d, om, ol, od, gm, gd, bd, wout, gc, wcq, kv, wco, tm, seq):
    t = x2d.shape[0]
    mem_len = kv.shape[1]
    row = lambda w: pl.BlockSpec((tm, w), lambda i, l1: (i, 0))
    full = lambda a: pl.BlockSpec(a.shape, lambda i, l1: (0,) * a.ndim)
    grid_spec = pltpu.PrefetchScalarGridSpec(
        num_scalar_prefetch=1,
        grid=(t // tm,),
        in_specs=[row(D_MODEL), row(om.shape[-1]), row(ol.shape[-1]), row(od.shape[-1]),
                  full(gm), full(gd), full(bd), full(wout), full(gc), full(wcq),
                  pl.BlockSpec((None, mem_len, 2 * D_MODEL), lambda i, l1: ((i * tm) // seq, 0, 0)),
                  full(wco)],
        out_specs=row(D_MODEL),
    )
    return pl.pallas_call(
        _mid_kernel,
        out_shape=jax.ShapeDtypeStruct((t, D_MODEL), F32),
        grid_spec=grid_spec,
        compiler_params=_cparams(("parallel",)),
        name="out_proj_cross_attn",
    )(lam1, x2d, om, ol, od, gm, gd, bd, wout, gc, wcq, kv, wco)


N_TOP = PEER_TOPK + 1
CAND_ROWS = 88


def _top_rows(s, scr, n):
    for k in range(n):
        m = jnp.max(s, axis=0, keepdims=True)
        scr[k:k + 1, :] = m
        s = jnp.where(s >= m, NEG, s)


def _score_kernel(x_ref, g_ref, wq_ref, key_ref, h_ref, eb_ref, ea_ref, thr_ref, a_scr, b_scr, c_scr):
    h3 = _rms(x_ref[...], g_ref[...]).astype(BF16)
    h_ref[...] = h3
    qp = _dot(h3, wq_ref[...])
    tm = qp.shape[0]
    for hd in range(PEER_HEADS):
        qh = qp[:, hd * LANE:(hd + 1) * LANE]
        s_a = _dot_nt(key_ref[0], qh, lax.Precision.HIGHEST)
        s_b = _dot_nt(key_ref[1], qh, lax.Precision.HIGHEST)
        _top_rows(s_a, a_scr, N_TOP)
        _top_rows(s_b, b_scr, N_TOP)
        a0 = a_scr[0:1, :]
        b0 = b_scr[0:1, :]
        c_scr[0:16, :] = a0 + b_scr[0:16, :]
        for p in range(1, 8):
            c_scr[8 + 8 * p:16 + 8 * p, :] = a_scr[p:p + 1, :] + b_scr[0:8, :]
        c_scr[72:80, :] = a_scr[8:16, :] + b0
        c_scr[80:81, :] = a0 + b_scr[16:17, :]
        c_scr[81:82, :] = a_scr[16:17, :] + b0
        c_scr[82:CAND_ROWS, :] = jnp.full((CAND_ROWS - 82, tm), NEG, F32)
        cand = c_scr[...]
        c = cand
        kth = None
        for k in range(N_TOP):
            m = jnp.max(c, axis=0, keepdims=True)
            if k == PEER_TOPK - 1:
                kth = m
            c = jnp.where(c >= m, NEG, c)
        tau = 0.5 * (kth + m)
        z = jnp.sum(jnp.where(cand > tau, jnp.exp(cand - (a0 + b0)), 0.0), axis=0, keepdims=True)
        eb_ref[hd] = jnp.exp(s_b - b0)
        ea_ref[hd] = jnp.exp(s_a - a0) / z
        thr_ref[hd] = jnp.exp(tau - s_a - b0)


def _score(x2, g, wq, keyext, tm):
    t = x2.shape[0]
    row = pl.BlockSpec((tm, D_MODEL), lambda i: (i, 0))
    full = lambda a: pl.BlockSpec(a.shape, lambda i: (0,) * a.ndim)
    kt = pl.BlockSpec((PEER_HEADS, PEER_NKEYS, tm), lambda i: (0, 0, i))
    kt_shape = jax.ShapeDtypeStruct((PEER_HEADS, PEER_NKEYS, t), F32)
    return pl.pallas_call(
        _score_kernel,
        out_shape=(jax.ShapeDtypeStruct((t, D_MODEL), BF16), kt_shape, kt_shape, kt_shape),
        grid=(t // tm,),
        in_specs=[row, full(g), full(wq), full(keyext)],
        out_specs=(row, kt, kt, kt),
        scratch_shapes=[pltpu.VMEM((24, tm), F32), pltpu.VMEM((24, tm), F32), pltpu.VMEM((CAND_ROWS, tm), F32)],
        compiler_params=_cparams(("parallel",)),
        name="peer_score",
    )(x2, g, wq, keyext)


def _peer_kernel(h_ref, x_ref, u_ref, vt_ref, eb_ref, ea_ref, thr_ref, o_ref, y_scr, a_scr, g_scr,
                 *, eblk, tm, nblk):
    s = pl.program_id(0)
    cur = s % 2
    jprev = (s + nblk - 1) % nblk

    @pl.when(s == 0)
    def _():
        a_scr[1] = jnp.zeros(a_scr.shape[1:], F32)
        y_scr[...] = jnp.zeros(y_scr.shape, F32)

    @pl.when(jprev == 0)
    def _():
        y_scr[...] = jnp.zeros(y_scr.shape, F32)

    a_scr[cur] = _dot_nt(u_ref[...], h_ref[...])
    nib = eblk // PEER_NKEYS
    for ii in range(nib):
        rows = slice(ii * PEER_NKEYS, (ii + 1) * PEER_NKEYS)
        for col in range(tm // LANE):
            cs = slice(col * LANE, (col + 1) * LANE)
            w = jnp.zeros((PEER_NKEYS, LANE), F32)
            for hd in range(PEER_HEADS):
                eb = eb_ref[hd, :, cs]
                w = w + jnp.where(eb > thr_ref[hd, ii:ii + 1, cs], eb * ea_ref[hd, ii:ii + 1, cs], 0.0)
            a = a_scr[1 - cur, rows, cs]
            g_scr[rows, cs] = (w * (0.5 * a * (1.0 + lax.erf(a * INV_SQRT2)))).astype(BF16)
    y_scr[...] += _dot(vt_ref[...], g_scr[...])

    @pl.when((jprev == nblk - 1) & (s > 0))
    def _():
        o_ref[...] = x_ref[...] + y_scr[...].T


def _peer(h3, x2, u, vt, eb, ea, thr, tm, eblk):
    t = x2.shape[0]
    nblk = PEER_EXPERTS // eblk
    total = (t // tm) * nblk
    cur_i = lambda s: jnp.minimum(s, total - 1) // nblk
    cur_j = lambda s: jnp.minimum(s, total - 1) % nblk
    prev_i = lambda s: jnp.maximum(s - 1, 0) // nblk
    prev_j = lambda s: jnp.maximum(s - 1, 0) % nblk
    row_prev = pl.BlockSpec((tm, D_MODEL), lambda s: (prev_i(s), 0))
    first_key_rows = pl.BlockSpec((PEER_HEADS, eblk // PEER_NKEYS, tm), lambda s: (0, prev_j(s), prev_i(s)))
    return pl.pallas_call(
        functools.partial(_peer_kernel, eblk=eblk, tm=tm, nblk=nblk),
        out_shape=jax.ShapeDtypeStruct((t, D_MODEL), F32),
        grid=(total + 1,),
        in_specs=[pl.BlockSpec((tm, D_MODEL), lambda s: (cur_i(s), 0)),
                  row_prev,
                  pl.BlockSpec((eblk, D_MODEL), lambda s: (cur_j(s), 0)),
                  pl.BlockSpec((D_MODEL, eblk), lambda s: (0, prev_j(s))),
                  pl.BlockSpec((PEER_HEADS, PEER_NKEYS, tm), lambda s: (0, 0, prev_i(s))),
                  first_key_rows, first_key_rows],
        out_specs=row_prev,
        scratch_shapes=[pltpu.VMEM((D_MODEL, tm), F32), pltpu.VMEM((2, eblk, tm), F32),
                        pltpu.VMEM((eblk, tm), BF16)],
        compiler_params=_cparams(("arbitrary",)),
        name="peer_mix",
    )(h3, x2, u, vt, eb, ea, thr)


def _final_kernel(x_ref, g_ref, o_ref):
    o_ref[...] = _rms(x_ref[...], g_ref[...])


def _final_norm(x2d, g, tm):
    t = x2d.shape[0]
    row = pl.BlockSpec((tm, D_MODEL), lambda i: (i, 0))
    return pl.pallas_call(
        _final_kernel,
        out_shape=jax.ShapeDtypeStruct((t, D_MODEL), F32),
        grid=(t // tm,),
        in_specs=[row, pl.BlockSpec((1, D_MODEL), lambda i: (0, 0))],
        out_specs=row,
        compiler_params=_cparams(("parallel",)),
        name="final_norm",
    )(x2d, g)


def _prep_weights(p):
    depth = p['w_in'].shape[0]
    w_in = p['w_in']
    z = lambda *s: jnp.zeros((depth,) + s, F32)
    c_q = w_in[..., 0:192]
    c_kv = w_in[..., 192:320]
    kr = w_in[..., 320:352]
    half = MLA_ROPE // 2
    kr_sw = jnp.concatenate([kr[..., half:], kr[..., :half]], axis=-1)
    pad_r = LANE - MLA_NOPE - MLA_ROPE
    place = lambda a: jnp.concatenate([z(D_MODEL, MLA_NOPE), a, z(D_MODEL, pad_r)], axis=-1)
    dq = w_in[..., 1376:1632].reshape(depth, D_MODEL, 2, 4, DIFF_DK)
    qd_exp = jnp.einsum('ldbjk,jm->ldbjmk', dq, jnp.eye(4, dtype=F32)).reshape(depth, D_MODEL, DIFF_MAPS * LANE)
    win = jnp.concatenate([c_q, z(D_MODEL, 256 - MLA_Q_RANK), c_kv, place(kr), place(kr_sw),
                           w_in[..., 352:864], w_in[..., 864:1376], qd_exp,
                           w_in[..., 1632:1888]], axis=-1).astype(BF16)
    wvdt = jnp.swapaxes(w_in[..., 1888:2144], 1, 2).astype(BF16)

    wuq = p['mla_w_uq'].reshape(depth, MLA_Q_RANK, MLA_HEADS, MLA_NOPE + MLA_ROPE)
    nope, rp = wuq[..., :MLA_NOPE], wuq[..., MLA_NOPE:]
    rp_sw = jnp.concatenate([rp[..., half:], rp[..., :half]], axis=-1)
    zq = lambda w: jnp.zeros((depth, MLA_Q_RANK, MLA_HEADS, w), F32)
    main = jnp.concatenate([nope, rp, zq(pad_r)], axis=-1).reshape(depth, MLA_Q_RANK, MLA_HEADS * LANE)
    swp = jnp.concatenate([zq(MLA_NOPE), rp_sw, zq(pad_r)], axis=-1).reshape(depth, MLA_Q_RANK, MLA_HEADS * LANE)
    wuq_ext = jnp.concatenate([main, swp], axis=-1)
    wuq_ext = jnp.concatenate([wuq_ext, jnp.zeros((depth, 256 - MLA_Q_RANK, wuq_ext.shape[-1]), F32)],
                              axis=1).astype(BF16)
    qn = jnp.concatenate([p['mla_q_norm'], jnp.zeros((depth, 256 - MLA_Q_RANK), F32)], axis=-1)[:, None, :]

    wukv = p['mla_w_ukv'].reshape(depth, MLA_KV_RANK, MLA_HEADS, MLA_NOPE + MLA_V)
    wk = jnp.concatenate([wukv[..., :MLA_NOPE], jnp.zeros((depth, MLA_KV_RANK, MLA_HEADS, LANE - MLA_NOPE), F32)],
                         axis=-1).reshape(depth, MLA_KV_RANK, MLA_HEADS * LANE).astype(BF16)
    wvt = jnp.swapaxes(wukv[..., MLA_NOPE:].reshape(depth, MLA_KV_RANK, MLA_HEADS * MLA_V), 1, 2).astype(BF16)

    eye_b = jnp.eye(LRU_BLOCKS, dtype=F32)
    bdiag = lambda w: jnp.einsum('lgij,gh->lgihj', w, eye_b).reshape(depth, LRU_WIDTH, LRU_WIDTH).astype(BF16)
    row = lambda a: a.reshape(depth, 1, -1)

    grp = np.arange(DIFF_HEADS * DIFF_DV) // DIFF_DV
    bd = jnp.asarray((grp[:, None] == grp[None, :]).astype(np.float32) / DIFF_DV)

    keys = p['peer_keys']
    kz = jnp.zeros_like(keys)
    keyext = jnp.stack([jnp.concatenate([keys[:, 0], kz[:, 0]], axis=-1),
                        jnp.concatenate([kz[:, 1], keys[:, 1]], axis=-1)], axis=1)

    lam_init = np.array([0.8 - 0.6 * math.exp(-0.3 * l) for l in range(depth)], np.float32)
    return dict(
        g_mix=row(p['norm_mix']), win=win, wvdt=wvdt, qn=qn, wuq=wuq_ext, kvn=row(p['mla_kv_norm']), wk=wk,
        wvt=wvt,
        cw=p['lru_conv_w'], cb=row(p['lru_conv_b']), wa=bdiag(p['lru_w_a']), ba=row(p['lru_b_a']),
        wx=bdiag(p['lru_w_x']), bx=row(p['lru_b_x']), lam=row(p['lru_lambda']), gn_lru=row(p['out_norm_lru']),
        dl=p['diff_lambda'], lam0=jnp.asarray(lam_init)[:, None], lam1=jnp.asarray(1.0 - lam_init)[:, None],
        gm=row(p['out_norm_mla']), gd=row(jnp.tile(p['diff_norm'], (1, DIFF_HEADS))),
        bd=jnp.broadcast_to(bd, (depth,) + bd.shape),
        wout=p['w_out'].astype(BF16), gc=row(p['norm_cross']), wcq=p['w_cq'].astype(BF16),
        wco=p['w_co'].astype(BF16), g_ffn=row(p['norm_ffn']), wq=p['peer_w_q'].astype(BF16), keyext=keyext,
        u=p['peer_u'].astype(BF16), vt=jnp.swapaxes(p['peer_v'], 1, 2).astype(BF16),
    )


def _forward(p, *, tm_in, ts_lru, t_attn, tm_mid, tm_score, tm_peer, eblk):
    x, mem, positions = p['x'], p['mem'], p['positions']
    b, s, d = x.shape
    t = b * s
    w = _prep_weights(p)
    ctab, stab = _rope_tables(positions, min(512, t))
    bt = _bias_tiles(p['rel_bias'], t_attn)
    kv_all = _mem_kv(mem.reshape(-1, d), p['norm_mem'], p['w_ckv'].astype(BF16))
    kv_all = kv_all.reshape(kv_all.shape[0], b, mem.shape[1], 2 * d)
    w['kv'] = kv_all

    def layer(x2d, lw):
        qm, qd, km, kd, vmt, vdt, xl, gl = _in_proj(x2d, lw['g_mix'], lw['win'], lw['wvdt'], lw['qn'], lw['wuq'],
                                                    lw['kvn'], lw['wk'], lw['wvt'], ctab, stab, tm_in)
        r3 = lambda a: a.reshape(b, s, a.shape[-1])
        ol = _lru(r3(xl), r3(gl), lw['cw'], lw['cb'], lw['wa'], lw['ba'], lw['wx'], lw['bx'], lw['lam'],
                  lw['gn_lru'], ts_lru)
        om, od = _attention(r3(qm), r3(qd), r3(km), r3(kd), vmt, vdt, bt, lw['dl'], lw['lam0'], t_attn)
        f2 = lambda a: a.reshape(t, a.shape[-1])
        x2 = _mid(lw['lam1'], x2d, f2(om), f2(ol), f2(od), lw['gm'], lw['gd'], lw['bd'], lw['wout'], lw['gc'],
                  lw['wcq'], lw['kv'], lw['wco'], tm_mid, s)
        h3, eb, ea, thr = _score(x2, lw['g_ffn'], lw['wq'], lw['keyext'], tm_score)
        x3 = _peer(h3, x2, lw['u'], lw['vt'], eb, ea, thr, tm_peer, eblk)
        return x3, None

    x2d, _ = lax.scan(layer, x.reshape(t, d), w)
    return _final_norm(x2d, p['norm_final'][None, :], min(512, t)).reshape(b, s, d)


def kernel(x, mem, positions, rel_bias, norm_mix, w_in, mla_q_norm, mla_w_uq, mla_kv_norm, mla_w_ukv, lru_conv_w, lru_conv_b, lru_w_a, lru_b_a, lru_w_x, lru_b_x, lru_lambda, diff_lambda, diff_norm, out_norm_mla, out_norm_lru, w_out, norm_cross, norm_mem, w_cq, w_ckv, w_co, norm_ffn, peer_w_q, peer_keys, peer_u, peer_v, norm_final):
    p = dict(x=x, mem=mem, positions=positions, rel_bias=rel_bias, norm_mix=norm_mix, w_in=w_in,
             mla_q_norm=mla_q_norm, mla_w_uq=mla_w_uq, mla_kv_norm=mla_kv_norm, mla_w_ukv=mla_w_ukv,
             lru_conv_w=lru_conv_w, lru_conv_b=lru_conv_b, lru_w_a=lru_w_a, lru_b_a=lru_b_a, lru_w_x=lru_w_x,
             lru_b_x=lru_b_x, lru_lambda=lru_lambda, diff_lambda=diff_lambda, diff_norm=diff_norm,
             out_norm_mla=out_norm_mla, out_norm_lru=out_norm_lru, w_out=w_out, norm_cross=norm_cross,
             norm_mem=norm_mem, w_cq=w_cq, w_ckv=w_ckv, w_co=w_co, norm_ffn=norm_ffn, peer_w_q=peer_w_q,
             peer_keys=peer_keys, peer_u=peer_u, peer_v=peer_v, norm_final=norm_final)
    return _forward(p, tm_in=512, ts_lru=256, t_attn=512, tm_mid=512, tm_score=256, tm_peer=512, eblk=1024)
```

```python
import functools
import math

import numpy as np
import jax
import jax.numpy as jnp
from jax import lax
from jax.experimental import pallas as pl
from jax.experimental.pallas import tpu as pltpu

F32 = jnp.float32
BF16 = jnp.bfloat16

D_MODEL = 1024
EPS = 1e-6
NEG = -1e30

MLA_HEADS = 4
MLA_NOPE = 64
MLA_ROPE = 32
MLA_V = 64
MLA_Q_RANK = 192
MLA_KV_RANK = 128
ROPE_THETA = 10000.0

LRU_WIDTH = 512
LRU_BLOCKS = 8
CONV_W = 4
LRU_C = 8.0

DIFF_HEADS = 4
DIFF_DK = 32
DIFF_DV = 64
DIFF_MAPS = 2 * DIFF_HEADS

N_BUCKETS = 32
MAX_DIST = 128

CROSS_HEADS = 4
CROSS_DH = D_MODEL // CROSS_HEADS

PEER_HEADS = 8
PEER_NKEYS = 128
PEER_EXPERTS = PEER_NKEYS * PEER_NKEYS
PEER_TOPK = 16

LANE = 128
MLA_SCALE = 1.0 / math.sqrt(MLA_NOPE + MLA_ROPE)
DIFF_SCALE = 1.0 / math.sqrt(DIFF_DK)
CROSS_SCALE = 1.0 / math.sqrt(CROSS_DH)
INV_SQRT2 = 1.0 / math.sqrt(2.0)
LOG2E = math.log2(math.e)

_C_CQ, _C_CKV, _C_KR, _C_KRS, _C_XL, _C_GL, _C_QD, _C_KD, _C_VD = (
    0, 256, 384, 512, 640, 1152, 1664, 2688, 2944)

VMEM_LIMIT = 56 * 1024 * 1024


def _cparams(sem):
    return pltpu.CompilerParams(dimension_semantics=sem, vmem_limit_bytes=VMEM_LIMIT)


def _rms(x, g, n=None):
    ms = jnp.sum(x * x, axis=-1, keepdims=True) * (1.0 / (n or x.shape[-1]))
    return x * lax.rsqrt(ms + EPS) * g


def _dot(a, b):
    return jnp.dot(a, b, preferred_element_type=F32)


def _dot_nt(a, b, precision=None):
    return lax.dot_general(a, b, (((1,), (1,)), ((), ())), preferred_element_type=F32,
                           precision=precision)


def _rope_kernel(pos_ref, inv_ref, c_ref, s_ref):
    ang = pos_ref[...].astype(F32) * inv_ref[...]
    lane = lax.broadcasted_iota(jnp.int32, ang.shape, 1)
    cos, sin = jnp.cos(ang), jnp.sin(ang)
    in_rope = (lane >= MLA_NOPE) & (lane < MLA_NOPE + MLA_ROPE)
    c_ref[...] = jnp.where(lane < MLA_NOPE, 1.0, jnp.where(in_rope, cos, 0.0))
    s_ref[...] = jnp.where(in_rope, jnp.where(lane < MLA_NOPE + MLA_ROPE // 2, -sin, sin), 0.0)


def _rope_tables(positions, tm):
    t = positions.size
    inv = ROPE_THETA ** (-jnp.arange(0, MLA_ROPE, 2, dtype=F32) / MLA_ROPE)
    inv_lane = jnp.concatenate([jnp.zeros((MLA_NOPE,), F32), inv, inv,
                                jnp.zeros((LANE - MLA_NOPE - MLA_ROPE,), F32)])[None, :]
    return pl.pallas_call(
        _rope_kernel,
        out_shape=(jax.ShapeDtypeStruct((t, LANE), F32),) * 2,
        grid=(t // tm,),
        in_specs=[pl.BlockSpec((tm, 1), lambda i: (i, 0)), pl.BlockSpec((1, LANE), lambda i: (0, 0))],
        out_specs=(pl.BlockSpec((tm, LANE), lambda i: (i, 0)),) * 2,
        compiler_params=_cparams(("parallel",)),
        name="rope_tables",
    )(positions.reshape(t, 1), inv_lane)


def _bias_kernel(rb_ref, o_ref, *, t):
    h = pl.program_id(0)
    mode = pl.program_id(1)
    kk = lax.broadcasted_iota(jnp.int32, (t, t), 0)
    qq = lax.broadcasted_iota(jnp.int32, (t, t), 1)
    rel = kk - qq - mode * t
    n = jnp.maximum(-rel, 0)
    max_exact = N_BUCKETS // 2
    large = max_exact + (jnp.log(jnp.maximum(n, 1).astype(F32) / max_exact)
                         / math.log(MAX_DIST / max_exact) * (N_BUCKETS - max_exact)).astype(jnp.int32)
    large = jnp.minimum(large, N_BUCKETS - 1)
    bucket = jnp.where(n < max_exact, n, large)

    def body(b, acc):
        return jnp.where(bucket == b, rb_ref[h * N_BUCKETS + b], acc)

    bias = lax.fori_loop(0, N_BUCKETS, body, jnp.zeros((t, t), F32))
    o_ref[...] = jnp.where(rel <= 0, bias * LOG2E, NEG)


def _bias_tiles(rel_bias, t):
    rb = jnp.concatenate([rel_bias.astype(F32).T, jnp.zeros((1, N_BUCKETS), F32)], axis=0).reshape(-1)
    return pl.pallas_call(
        functools.partial(_bias_kernel, t=t),
        out_shape=jax.ShapeDtypeStruct((DIFF_HEADS + 1, 2, t, t), F32),
        grid=(DIFF_HEADS + 1, 2),
        in_specs=[pl.BlockSpec(memory_space=pltpu.SMEM)],
        out_specs=pl.BlockSpec((None, None, t, t), lambda h, m: (h, m, 0, 0)),
        compiler_params=_cparams(("parallel", "parallel")),
        name="bias_tiles",
    )(rb)


def _memkv_kernel(mem_ref, g_ref, w_ref, kv_ref):
    h = _rms(mem_ref[...], g_ref[...]).astype(BF16)
    kv_ref[...] = _dot(h, w_ref[...]).astype(BF16)


def _mem_kv(mem2d, norm_mem, w_ckv_bf):
    depth = norm_mem.shape[0]
    rows = mem2d.shape[0]
    return pl.pallas_call(
        _memkv_kernel,
        out_shape=jax.ShapeDtypeStruct((depth, rows, 2 * D_MODEL), BF16),
        grid=(depth,),
        in_specs=[pl.BlockSpec((rows, D_MODEL), lambda l: (0, 0)),
                  pl.BlockSpec((None, 1, D_MODEL), lambda l: (l, 0, 0)),
                  pl.BlockSpec((None, D_MODEL, 2 * D_MODEL), lambda l: (l, 0, 0))],
        out_specs=pl.BlockSpec((None, rows, 2 * D_MODEL), lambda l: (l, 0, 0)),
        compiler_params=_cparams(("parallel",)),
        name="mem_kv",
    )(mem2d, norm_mem.reshape(depth, 1, D_MODEL), w_ckv_bf)


def _in_kernel(x_ref, g_ref, win_ref, wvdt_ref, qn_ref, wuq_ref, kvn_ref, wk_ref, wvt_ref, c_ref, s_ref,
               qm_ref, qd_ref, km_ref, kd_ref, vmt_ref, vdt_ref, xl_ref, gl_ref):
    h = _rms(x_ref[...], g_ref[...]).astype(BF16)

    def proj(a, b):
        return _dot(h, win_ref[:, a:b])

    c = c_ref[...]
    sg = s_ref[...]
    cq = proj(_C_CQ, _C_CKV)
    cqn = _rms(cq, qn_ref[...], MLA_Q_RANK).astype(BF16)
    qq = _dot(cqn, wuq_ref[...])
    half = MLA_HEADS * LANE
    for hd in range(MLA_HEADS):
        blk = slice(hd * LANE, (hd + 1) * LANE)
        swp = slice(half + hd * LANE, half + (hd + 1) * LANE)
        qm_ref[:, blk] = ((qq[:, blk] * c + qq[:, swp] * sg) * (MLA_SCALE * LOG2E)).astype(BF16)
    ckvn = _rms(proj(_C_CKV, _C_KR), kvn_ref[...]).astype(BF16)
    kn = _dot(ckvn, wk_ref[...])
    kr = proj(_C_KR, _C_KRS) * c + proj(_C_KRS, _C_XL) * sg
    for hd in range(MLA_HEADS):
        blk = slice(hd * LANE, (hd + 1) * LANE)
        km_ref[:, blk] = (kn[:, blk] + kr).astype(BF16)
    vmt_ref[...] = _dot_nt(wvt_ref[...], ckvn).astype(BF16)
    vdt_ref[...] = _dot_nt(wvdt_ref[...], h).astype(BF16)
    xl_ref[...] = proj(_C_XL, _C_GL)
    gl_ref[...] = proj(_C_GL, _C_QD)
    qd_ref[...] = (proj(_C_QD, _C_KD) * (DIFF_SCALE * LOG2E)).astype(BF16)
    kd_ref[...] = proj(_C_KD, _C_VD).astype(BF16)


def _in_proj(x2d, g, win, wvdt, qn, wuq, kvn, wk, wvt, ctab, stab, tm):
    t = x2d.shape[0]
    row = lambda w: pl.BlockSpec((tm, w), lambda i: (i, 0))
    col = lambda w: pl.BlockSpec((w, tm), lambda i: (0, i))
    full = lambda a: pl.BlockSpec(a.shape, lambda i: (0,) * a.ndim)
    vw = MLA_HEADS * MLA_V
    shapes = [((t, MLA_HEADS * LANE), BF16, row), ((t, DIFF_MAPS * LANE), BF16, row),
              ((t, MLA_HEADS * LANE), BF16, row), ((t, DIFF_MAPS * DIFF_DK), BF16, row),
              ((vw, t), BF16, col), ((vw, t), BF16, col), ((t, LRU_WIDTH), F32, row), ((t, LRU_WIDTH), F32, row)]
    return pl.pallas_call(
        _in_kernel,
        out_shape=tuple(jax.ShapeDtypeStruct(sh, dt) for sh, dt, _ in shapes),
        grid=(t // tm,),
        in_specs=[row(D_MODEL), full(g), full(win), full(wvdt), full(qn), full(wuq), full(kvn), full(wk),
                  full(wvt), row(LANE), row(LANE)],
        out_specs=tuple(mk(sh[1] if mk is row else sh[0]) for sh, _, mk in shapes),
        compiler_params=_cparams(("parallel",)),
        name="in_proj",
    )(x2d, g, win, wvdt, qn, wuq, kvn, wk, wvt, ctab, stab)


def _lru_kernel(xl_ref, gl_ref, cw_ref, cb_ref, wa_ref, ba_ref, wx_ref, bx_ref, lam_ref, gn_ref,
                o_ref, xbuf, hprev, *, ts):
    @pl.when(pl.program_id(1) == 0)
    def _():
        xbuf[0:8, :] = jnp.zeros((8, LRU_WIDTH), F32)
        hprev[...] = jnp.zeros_like(hprev)

    xbuf[8:8 + ts, :] = xl_ref[...]
    xc = cb_ref[...] + jnp.zeros((ts, LRU_WIDTH), F32)
    for k in range(CONV_W):
        off = 8 - (CONV_W - 1) + k
        xc = xc + xbuf[off:off + ts, :] * cw_ref[k:k + 1, :]
    xbuf[0:8, :] = xbuf[ts:ts + 8, :]

    xcb = xc.astype(BF16)
    r = jax.nn.sigmoid(_dot(xcb, wa_ref[...]) + ba_ref[...])
    i = jax.nn.sigmoid(_dot(xcb, wx_ref[...]) + bx_ref[...])
    nl = -lam_ref[...]
    softplus = jnp.maximum(nl, 0.0) + jnp.log1p(jnp.exp(-jnp.abs(nl)))
    log_a = -LRU_C * r * softplus
    a = jnp.exp(log_a)
    th = jnp.tanh(log_a)
    one_minus_a2 = -2.0 * th / (1.0 - th)
    u = jnp.sqrt(one_minus_a2) * (i * xc)

    row = lax.broadcasted_iota(jnp.int32, (ts, LRU_WIDTH), 0)
    d = 1
    while d < ts:
        keep = row >= d
        a_sh = jnp.where(keep, pltpu.roll(a, d, 0), 1.0)
        u_sh = jnp.where(keep, pltpu.roll(u, d, 0), 0.0)
        u = a * u_sh + u
        a = a * a_sh
        d *= 2
    hs = a * hprev[0:1, :] + u
    hprev[0:1, :] = hs[ts - 1:ts, :]

    gl = gl_ref[...]
    gate = 0.5 * gl * (1.0 + lax.erf(gl * INV_SQRT2))
    o_ref[...] = _rms(hs * gate, gn_ref[...]).astype(BF16)


def _lru(xl, gl, cw, cb, wa, ba, wx, bx, lam, gn, ts):
    b, s, _ = xl.shape
    seq = pl.BlockSpec((None, ts, LRU_WIDTH), lambda bi, ci: (bi, ci, 0))
    full = lambda a: pl.BlockSpec(a.shape, lambda bi, ci: (0,) * a.ndim)
    return pl.pallas_call(
        functools.partial(_lru_kernel, ts=ts),
        out_shape=jax.ShapeDtypeStruct((b, s, LRU_WIDTH), BF16),
        grid=(b, s // ts),
        in_specs=[seq, seq, full(cw), full(cb), full(wa), full(ba), full(wx), full(bx), full(lam), full(gn)],
        out_specs=seq,
        scratch_shapes=[pltpu.VMEM((ts + 8, LRU_WIDTH), F32), pltpu.VMEM((8, LRU_WIDTH), F32)],
        compiler_params=_cparams(("parallel", "arbitrary")),
        name="rg_lru",
    )(xl, gl, cw, cb, wa, ba, wx, bx, lam, gn)


N_MAPS = MLA_HEADS + DIFF_MAPS


SUB = 8
BF16_ROWS = 16
KEY_WORD_ROWS = PEER_NKEYS // 2
TILE_DIAG, TILE_NEAR, TILE_FAR = 0, 1, 2


def _attn_kernel(qt_ref, kt_ref, mt_ref, fs_ref, qm_ref, qd_ref, km_ref, kd_ref, vmt_ref, vdt_ref, bt_ref,
                 dl_ref, om_ref, od_ref, m_scr, l_scr, acc_scr):
    step = pl.program_id(1)
    kblk = kt_ref[step]
    mode = mt_ref[step]
    tk, tq = km_ref.shape[0], qm_ref.shape[0]

    @pl.when(kblk == 0)
    def _():
        m_scr[...] = jnp.full(m_scr.shape, NEG, F32)
        l_scr[...] = jnp.zeros(l_scr.shape, F32)
        acc_scr[...] = jnp.zeros(acc_scr.shape, F32)

    def run_maps(tile):
        for m in range(N_MAPS):
            mla = m < MLA_HEADS
            if mla:
                hd = m
                q = qm_ref[:, m * LANE:(m + 1) * LANE]
                k = km_ref[:, m * LANE:(m + 1) * LANE]
                vt = vmt_ref[hd * MLA_V:(hd + 1) * MLA_V, :]
            else:
                dm = m - MLA_HEADS
                hd = dm // 2
                q = qd_ref[:, dm * LANE:(dm + 1) * LANE]
                k = kd_ref[:, (dm // 4) * LANE:(dm // 4 + 1) * LANE]
                vt = vdt_ref[hd * DIFF_DV:(hd + 1) * DIFF_DV, :]
            s = _dot_nt(k, q)
            shift = None
            if tile == TILE_DIAG:
                s = s + bt_ref[DIFF_HEADS if mla else hd, 0]
            elif tile == TILE_NEAR and not mla:
                s = s + bt_ref[hd, 1]
            elif tile == TILE_FAR and not mla:
                shift = fs_ref[1 + hd]
            s3 = s.reshape(tk // SUB, SUB, tq)
            m_prev = m_scr[m]
            m_cur = jnp.max(jnp.max(s3, axis=0), axis=0, keepdims=True)
            if shift is not None:
                m_cur = m_cur + shift
            m_new = jnp.maximum(m_prev, m_cur)
            alpha = jnp.exp2(m_prev - m_new)
            p = jnp.exp2(s3 - (m_new if shift is None else m_new - shift)[None])
            l_scr[m] = alpha * l_scr[m] + jnp.sum(p, axis=0)
            pv = _dot(vt, p.reshape(tk, tq).astype(BF16))
            nv = pv.shape[0] // SUB
            acc_scr[m] = (alpha[None] * acc_scr[m].reshape(nv, SUB, tq)).reshape(pv.shape) + pv
            m_scr[m] = m_new

    @pl.when(mode == TILE_FAR)
    def _():
        run_maps(TILE_FAR)

    @pl.when(mode == TILE_NEAR)
    def _():
        run_maps(TILE_NEAR)

    @pl.when(mode == TILE_DIAG)
    def _():
        run_maps(TILE_DIAG)
        dl = dl_ref[...]
        lam = (jnp.exp(jnp.sum(dl[0:1] * dl[1:2], axis=-1, keepdims=True))
               - jnp.exp(jnp.sum(dl[2:3] * dl[3:4], axis=-1, keepdims=True)) + fs_ref[0])

        def head_rows(m):
            return acc_scr[m] / jnp.sum(l_scr[m], axis=0, keepdims=True)

        o_mla = [head_rows(hd) for hd in range(MLA_HEADS)]
        o_diff = [head_rows(MLA_HEADS + 2 * hd) - lam * head_rows(MLA_HEADS + 2 * hd + 1)
                  for hd in range(DIFF_HEADS)]
        om_ref[...] = jnp.concatenate(o_mla, axis=0).T
        od_ref[...] = jnp.concatenate(o_diff, axis=0).T


def _attention(qm, qd, km, kd, vmt, vdt, bt, dl, fs, t):
    b, s, _ = qm.shape
    nq = s // t
    assert t >= MAX_DIST, "tiles two or more blocks off the diagonal must lie wholly in the last T5 bucket"
    qtab = np.array([q for q in range(nq) for k in range(q + 1)], np.int32)
    ktab = np.array([k for q in range(nq) for k in range(q + 1)], np.int32)
    mtab = np.minimum(qtab - ktab, 2).astype(np.int32)
    qspec = lambda w: pl.BlockSpec((None, t, w), lambda bi, si, qt, kt, mt, l0: (bi, qt[si], 0))
    kspec = lambda w: pl.BlockSpec((None, t, w), lambda bi, si, qt, kt, mt, l0: (bi, kt[si], 0))
    vspec = lambda w: pl.BlockSpec((w, t), lambda bi, si, qt, kt, mt, l0: (0, bi * nq + kt[si]))
    full = lambda a: pl.BlockSpec(a.shape, lambda bi, si, qt, kt, mt, l0: (0,) * a.ndim)
    ow = MLA_HEADS * MLA_V
    grid_spec = pltpu.PrefetchScalarGridSpec(
        num_scalar_prefetch=4,
        grid=(b, len(qtab)),
        in_specs=[qspec(qm.shape[-1]), qspec(qd.shape[-1]), kspec(km.shape[-1]), kspec(kd.shape[-1]),
                  vspec(vmt.shape[0]), vspec(vdt.shape[0]), full(bt), full(dl)],
        out_specs=(qspec(ow), qspec(ow)),
        scratch_shapes=[pltpu.VMEM((N_MAPS, SUB, t), F32), pltpu.VMEM((N_MAPS, SUB, t), F32),
                        pltpu.VMEM((N_MAPS, MLA_V, t), F32)],
    )
    return pl.pallas_call(
        _attn_kernel,
        out_shape=(jax.ShapeDtypeStruct((b, s, ow), F32),) * 2,
        grid_spec=grid_spec,
        compiler_params=_cparams(("parallel", "arbitrary")),
        name="attention",
    )(jnp.asarray(qtab), jnp.asarray(ktab), jnp.asarray(mtab), fs, qm, qd, km, kd, vmt, vdt, bt, dl)


def _mid_kernel(lam1_ref, x_ref, om_ref, ol_ref, od_ref, gm_ref, gd_ref, bd_ref, wout_ref, gc_ref, wcq_ref,
                kv_ref, wco_ref, x2_ref):
    omn = _rms(om_ref[...], gm_ref[...]).astype(BF16)
    od = od_ref[...]
    ms = jnp.dot(od * od, bd_ref[...], preferred_element_type=F32, precision=lax.Precision.HIGHEST)
    odn = (od * lax.rsqrt(ms + EPS) * gd_ref[...] * lam1_ref[0]).astype(BF16)
    w0, w1 = MLA_HEADS * MLA_V, MLA_HEADS * MLA_V + LRU_WIDTH
    y = (_dot(omn, wout_ref[0:w0, :]) + _dot(ol_ref[...], wout_ref[w0:w1, :])
         + _dot(odn, wout_ref[w1:, :]))
    x1 = x_ref[...] + y
    h2 = _rms(x1, gc_ref[...]).astype(BF16)
    q = (_dot(h2, wcq_ref[...]) * CROSS_SCALE).astype(BF16)
    outs = []
    for hd in range(CROSS_HEADS):
        sl = slice(hd * CROSS_DH, (hd + 1) * CROSS_DH)
        s = _dot_nt(q[:, sl], kv_ref[:, sl])
        p = jnp.exp(s - jnp.max(s, axis=-1, keepdims=True))
        p = (p / jnp.sum(p, axis=-1, keepdims=True)).astype(BF16)
        outs.append(_dot(p, kv_ref[:, D_MODEL + hd * CROSS_DH:D_MODEL + (hd + 1) * CROSS_DH]).astype(BF16))
    o = jnp.concatenate(outs, axis=-1)
    x2_ref[...] = x1 + _dot(o, wco_ref[...])


def _mid(lam1, x2d, om, ol, od, gm, gd, bd, wout, gc, wcq, kv, wco, tm, seq):
    t = x2d.shape[0]
    mem_len = kv.shape[1]
    row = lambda w: pl.BlockSpec((tm, w), lambda i, l1: (i, 0))
    full = lambda a: pl.BlockSpec(a.shape, lambda i, l1: (0,) * a.ndim)
    grid_spec = pltpu.PrefetchScalarGridSpec(
        num_scalar_prefetch=1,
        grid=(t // tm,),
        in_specs=[row(D_MODEL), row(om.shape[-1]), row(ol.shape[-1]), row(od.shape[-1]),
                  full(gm), full(gd), full(bd), full(wout), full(gc), full(wcq),
                  pl.BlockSpec((None, mem_len, 2 * D_MODEL), lambda i, l1: ((i * tm) // seq, 0, 0)),
                  full(wco)],
        out_specs=row(D_MODEL),
    )
    return pl.pallas_call(
        _mid_kernel,
        out_shape=jax.ShapeDtypeStruct((t, D_MODEL), F32),
        grid_spec=grid_spec,
        compiler_params=_cparams(("parallel",)),
        name="out_proj_cross_attn",
    )(lam1, x2d, om, ol, od, gm, gd, bd, wout, gc, wcq, kv, wco)


N_TOP = PEER_TOPK + 1
CAND_ROWS = 88


def _top_rows(s, scr, n):
    for k in range(n):
        m = jnp.max(s, axis=0, keepdims=True)
        scr[k:k + 1, :] = m
        s = jnp.where(s >= m, NEG, s)


def _bf16_pair_words(x):
    bits = pltpu.bitcast(x.astype(BF16).astype(F32), jnp.uint32)
    return (bits & jnp.uint32(0xFFFF0000)) | (bits >> 16)


def _top_rows_ranked(s, scr, n):
    rank = jnp.full(s.shape, float(n), F32)
    for k in range(n):
        m = jnp.max(s, axis=0, keepdims=True)
        scr[k:k + 1, :] = m
        hit = s >= m
        rank = jnp.where(hit, float(k), rank)
        s = jnp.where(hit, NEG, s)
    return rank


def _score_kernel(x_ref, g_ref, wq_ref, key_ref, h_ref, eb_ref, rk_ref, ea_ref, cnt_ref, a_scr, b_scr, c_scr):
    h3 = _rms(x_ref[...], g_ref[...]).astype(BF16)
    h_ref[...] = h3
    qp = _dot(h3, wq_ref[...])
    tm = qp.shape[0]
    for hd in range(PEER_HEADS):
        qh = qp[:, hd * LANE:(hd + 1) * LANE]
        s_a = _dot_nt(key_ref[0], qh, lax.Precision.HIGHEST)
        s_b = _dot_nt(key_ref[1], qh, lax.Precision.HIGHEST)
        _top_rows(s_a, a_scr, N_TOP)
        rank_b = _top_rows_ranked(s_b, b_scr, N_TOP)
        a0 = a_scr[0:1, :]
        b0 = b_scr[0:1, :]
        c_scr[0:16, :] = a0 + b_scr[0:16, :]
        for p in range(1, 8):
            c_scr[8 + 8 * p:16 + 8 * p, :] = a_scr[p:p + 1, :] + b_scr[0:8, :]
        c_scr[72:80, :] = a_scr[8:16, :] + b0
        c_scr[80:81, :] = a0 + b_scr[16:17, :]
        c_scr[81:82, :] = a_scr[16:17, :] + b0
        c_scr[82:CAND_ROWS, :] = jnp.full((CAND_ROWS - 82, tm), NEG, F32)
        cand = c_scr[...]
        c = cand
        kth = None
        for k in range(N_TOP):
            m = jnp.max(c, axis=0, keepdims=True)
            if k == PEER_TOPK - 1:
                kth = m
            c = jnp.where(c >= m, NEG, c)
        tau = 0.5 * (kth + m)
        z = jnp.sum(jnp.where(cand > tau, jnp.exp(cand - (a0 + b0)), 0.0), axis=0, keepdims=True)
        theta = tau - s_a
        cnt = jnp.zeros(s_a.shape, F32)
        for q in range(PEER_TOPK):
            cnt = jnp.where(b_scr[q:q + 1, :] > theta, float(q + 1), cnt)
        words = slice(hd * KEY_WORD_ROWS, (hd + 1) * KEY_WORD_ROWS)
        eb_ref[words, :] = pltpu.bitcast(jnp.exp(s_b - b0).astype(BF16), jnp.uint32)
        rk_ref[words, :] = pltpu.bitcast(rank_b.astype(BF16), jnp.uint32)
        ea_ref[hd] = _bf16_pair_words(jnp.exp(s_a - a0) * (0.5 / z))
        cnt_ref[hd] = _bf16_pair_words(cnt)


def _score(x2, g, wq, keyext, tm):
    t = x2.shape[0]
    row = pl.BlockSpec((tm, D_MODEL), lambda i: (i, 0))
    full = lambda a: pl.BlockSpec(a.shape, lambda i: (0,) * a.ndim)
    kt = pl.BlockSpec((PEER_HEADS, PEER_NKEYS, tm), lambda i: (0, 0, i))
    kt_f32 = jax.ShapeDtypeStruct((PEER_HEADS, PEER_NKEYS, t), jnp.uint32)
    kt2 = pl.BlockSpec((PEER_HEADS * KEY_WORD_ROWS, tm), lambda i: (0, i))
    kt_bf16 = jax.ShapeDtypeStruct((PEER_HEADS * KEY_WORD_ROWS, t), jnp.uint32)
    return pl.pallas_call(
        _score_kernel,
        out_shape=(jax.ShapeDtypeStruct((t, D_MODEL), BF16), kt_bf16, kt_bf16, kt_f32, kt_f32),
        grid=(t // tm,),
        in_specs=[row, full(g), full(wq), full(keyext)],
        out_specs=(row, kt2, kt2, kt, kt),
        scratch_shapes=[pltpu.VMEM((24, tm), F32), pltpu.VMEM((24, tm), F32), pltpu.VMEM((CAND_ROWS, tm), F32)],
        compiler_params=_cparams(("parallel",)),
        name="peer_score",
    )(x2, g, wq, keyext)


def _peer_kernel(h_ref, x_ref, u_ref, vt_ref, eb_ref, rk_ref, ea_ref, cnt_ref, o_ref, y_scr, a_scr, g_scr,
                 *, eblk, tm, nblk):
    s = pl.program_id(0)
    cur = s % 2
    jprev = (s + nblk - 1) % nblk

    @pl.when(s == 0)
    def _():
        a_scr[1] = jnp.zeros(a_scr.shape[1:], F32)
        y_scr[...] = jnp.zeros(y_scr.shape, F32)

    @pl.when(jprev == 0)
    def _():
        y_scr[...] = jnp.zeros(y_scr.shape, F32)

    a_scr[cur] = _dot_nt(u_ref[...], h_ref[...])
    nib = eblk // PEER_NKEYS
    for ii in range(nib):
        rows = slice(ii * PEER_NKEYS, (ii + 1) * PEER_NKEYS)
        for col in range(tm // LANE):
            cs = slice(col * LANE, (col + 1) * LANE)
            packed = (PEER_NKEYS // BF16_ROWS, BF16_ROWS, LANE)
            w = jnp.zeros(packed, BF16)
            for hd in range(PEER_HEADS):
                cnt = pltpu.bitcast(jnp.broadcast_to(cnt_ref[hd, ii:ii + 1, cs], (SUB, LANE)), BF16)
                ea = pltpu.bitcast(jnp.broadcast_to(ea_ref[hd, ii:ii + 1, cs], (SUB, LANE)), BF16)
                words = slice(hd * KEY_WORD_ROWS, (hd + 1) * KEY_WORD_ROWS)
                rk = pltpu.bitcast(rk_ref[words, cs], BF16).reshape(packed)
                eb = pltpu.bitcast(eb_ref[words, cs], BF16).reshape(packed)
                w = w + jnp.where(rk < cnt[None], eb * ea[None], jnp.zeros((), BF16))
            a = a_scr[1 - cur, rows, cs]
            gelu2 = (a + a * lax.erf(a * INV_SQRT2)).astype(BF16)
            g_scr[rows, cs] = w.reshape(PEER_NKEYS, LANE) * gelu2
    y_scr[...] += _dot(vt_ref[...], g_scr[...])

    @pl.when((jprev == nblk - 1) & (s > 0))
    def _():
        o_ref[...] = x_ref[...] + y_scr[...].T


def _peer(h3, x2, u, vt, eb, rk, ea, cnt, tm, eblk):
    t = x2.shape[0]
    nblk = PEER_EXPERTS // eblk
    total = (t // tm) * nblk
    cur_i = lambda s: jnp.minimum(s, total - 1) // nblk
    cur_j = lambda s: jnp.minimum(s, total - 1) % nblk
    prev_i = lambda s: jnp.maximum(s - 1, 0) // nblk
    prev_j = lambda s: jnp.maximum(s - 1, 0) % nblk
    row_prev = pl.BlockSpec((tm, D_MODEL), lambda s: (prev_i(s), 0))
    first_key_rows = pl.BlockSpec((PEER_HEADS, eblk // PEER_NKEYS, tm), lambda s: (0, prev_j(s), prev_i(s)))
    second_key_tile = pl.BlockSpec((PEER_HEADS * KEY_WORD_ROWS, tm), lambda s: (0, prev_i(s)))
    return pl.pallas_call(
        functools.partial(_peer_kernel, eblk=eblk, tm=tm, nblk=nblk),
        out_shape=jax.ShapeDtypeStruct((t, D_MODEL), F32),
        grid=(total + 1,),
        in_specs=[pl.BlockSpec((tm, D_MODEL), lambda s: (cur_i(s), 0)),
                  row_prev,
                  pl.BlockSpec((eblk, D_MODEL), lambda s: (cur_j(s), 0)),
                  pl.BlockSpec((D_MODEL, eblk), lambda s: (0, prev_j(s))),
                  second_key_tile, second_key_tile, first_key_rows, first_key_rows],
        out_specs=row_prev,
        scratch_shapes=[pltpu.VMEM((D_MODEL, tm), F32), pltpu.VMEM((2, eblk, tm), F32),
                        pltpu.VMEM((eblk, tm), BF16)],
        compiler_params=_cparams(("arbitrary",)),
        name="peer_mix",
    )(h3, x2, u, vt, eb, rk, ea, cnt)


def _final_kernel(x_ref, g_ref, o_ref):
    o_ref[...] = _rms(x_ref[...], g_ref[...])


def _final_norm(x2d, g, tm):
    t = x2d.shape[0]
    row = pl.BlockSpec((tm, D_MODEL), lambda i: (i, 0))
    return pl.pallas_call(
        _final_kernel,
        out_shape=jax.ShapeDtypeStruct((t, D_MODEL), F32),
        grid=(t // tm,),
        in_specs=[row, pl.BlockSpec((1, D_MODEL), lambda i: (0, 0))],
        out_specs=row,
        compiler_params=_cparams(("parallel",)),
        name="final_norm",
    )(x2d, g)


def _prep_weights(p):
    depth = p['w_in'].shape[0]
    w_in = p['w_in']
    z = lambda *s: jnp.zeros((depth,) + s, F32)
    c_q = w_in[..., 0:192]
    c_kv = w_in[..., 192:320]
    kr = w_in[..., 320:352]
    half = MLA_ROPE // 2
    kr_sw = jnp.concatenate([kr[..., half:], kr[..., :half]], axis=-1)
    pad_r = LANE - MLA_NOPE - MLA_ROPE
    place = lambda a: jnp.concatenate([z(D_MODEL, MLA_NOPE), a, z(D_MODEL, pad_r)], axis=-1)
    dq = w_in[..., 1376:1632].reshape(depth, D_MODEL, 2, 4, DIFF_DK)
    qd_exp = jnp.einsum('ldbjk,jm->ldbjmk', dq, jnp.eye(4, dtype=F32)).reshape(depth, D_MODEL, DIFF_MAPS * LANE)
    win = jnp.concatenate([c_q, z(D_MODEL, 256 - MLA_Q_RANK), c_kv, place(kr), place(kr_sw),
                           w_in[..., 352:864], w_in[..., 864:1376], qd_exp,
                           w_in[..., 1632:1888]], axis=-1).astype(BF16)
    wvdt = jnp.swapaxes(w_in[..., 1888:2144], 1, 2).astype(BF16)

    wuq = p['mla_w_uq'].reshape(depth, MLA_Q_RANK, MLA_HEADS, MLA_NOPE + MLA_ROPE)
    nope, rp = wuq[..., :MLA_NOPE], wuq[..., MLA_NOPE:]
    rp_sw = jnp.concatenate([rp[..., half:], rp[..., :half]], axis=-1)
    zq = lambda w: jnp.zeros((depth, MLA_Q_RANK, MLA_HEADS, w), F32)
    main = jnp.concatenate([nope, rp, zq(pad_r)], axis=-1).reshape(depth, MLA_Q_RANK, MLA_HEADS * LANE)
    swp = jnp.concatenate([zq(MLA_NOPE), rp_sw, zq(pad_r)], axis=-1).reshape(depth, MLA_Q_RANK, MLA_HEADS * LANE)
    wuq_ext = jnp.concatenate([main, swp], axis=-1)
    wuq_ext = jnp.concatenate([wuq_ext, jnp.zeros((depth, 256 - MLA_Q_RANK, wuq_ext.shape[-1]), F32)],
                              axis=1).astype(BF16)
    qn = jnp.concatenate([p['mla_q_norm'], jnp.zeros((depth, 256 - MLA_Q_RANK), F32)], axis=-1)[:, None, :]

    wukv = p['mla_w_ukv'].reshape(depth, MLA_KV_RANK, MLA_HEADS, MLA_NOPE + MLA_V)
    wk = jnp.concatenate([wukv[..., :MLA_NOPE], jnp.zeros((depth, MLA_KV_RANK, MLA_HEADS, LANE - MLA_NOPE), F32)],
                         axis=-1).reshape(depth, MLA_KV_RANK, MLA_HEADS * LANE).astype(BF16)
    wvt = jnp.swapaxes(wukv[..., MLA_NOPE:].reshape(depth, MLA_KV_RANK, MLA_HEADS * MLA_V), 1, 2).astype(BF16)

    eye_b = jnp.eye(LRU_BLOCKS, dtype=F32)
    bdiag = lambda w: jnp.einsum('lgij,gh->lgihj', w, eye_b).reshape(depth, LRU_WIDTH, LRU_WIDTH).astype(BF16)
    row = lambda a: a.reshape(depth, 1, -1)

    grp = np.arange(DIFF_HEADS * DIFF_DV) // DIFF_DV
    bd = jnp.asarray((grp[:, None] == grp[None, :]).astype(np.float32) / DIFF_DV)

    keys = p['peer_keys']
    kz = jnp.zeros_like(keys)
    keyext = jnp.stack([jnp.concatenate([keys[:, 0], kz[:, 0]], axis=-1),
                        jnp.concatenate([kz[:, 1], keys[:, 1]], axis=-1)], axis=1)

    lam_init = np.array([0.8 - 0.6 * math.exp(-0.3 * l) for l in range(depth)], np.float32)
    return dict(
        g_mix=row(p['norm_mix']), win=win, wvdt=wvdt, qn=qn, wuq=wuq_ext, kvn=row(p['mla_kv_norm']), wk=wk,
        wvt=wvt,
        cw=p['lru_conv_w'], cb=row(p['lru_conv_b']), wa=bdiag(p['lru_w_a']), ba=row(p['lru_b_a']),
        wx=bdiag(p['lru_w_x']), bx=row(p['lru_b_x']), lam=row(p['lru_lambda']), gn_lru=row(p['out_norm_lru']),
        dl=p['diff_lambda'], lam0=jnp.asarray(lam_init)[:, None], lam1=jnp.asarray(1.0 - lam_init)[:, None],
        gm=row(p['out_norm_mla']), gd=row(jnp.tile(p['diff_norm'], (1, DIFF_HEADS))),
        bd=jnp.broadcast_to(bd, (depth,) + bd.shape),
        wout=p['w_out'].astype(BF16), gc=row(p['norm_cross']), wcq=p['w_cq'].astype(BF16),
        wco=p['w_co'].astype(BF16), g_ffn=row(p['norm_ffn']), wq=p['peer_w_q'].astype(BF16), keyext=keyext,
        u=p['peer_u'].astype(BF16), vt=jnp.swapaxes(p['peer_v'], 1, 2).astype(BF16),
    )


def _forward(p, *, tm_in, ts_lru, t_attn, tm_mid, tm_score, tm_peer, eblk):
    x, mem, positions = p['x'], p['mem'], p['positions']
    b, s, d = x.shape
    t = b * s
    w = _prep_weights(p)
    ctab, stab = _rope_tables(positions, min(512, t))
    bt = _bias_tiles(p['rel_bias'], t_attn)
    far_bias = p['rel_bias'][N_BUCKETS - 1].astype(F32) * LOG2E
    kv_all = _mem_kv(mem.reshape(-1, d), p['norm_mem'], p['w_ckv'].astype(BF16))
    kv_all = kv_all.reshape(kv_all.shape[0], b, mem.shape[1], 2 * d)
    w['kv'] = kv_all

    def layer(x2d, lw):
        qm, qd, km, kd, vmt, vdt, xl, gl = _in_proj(x2d, lw['g_mix'], lw['win'], lw['wvdt'], lw['qn'], lw['wuq'],
                                                    lw['kvn'], lw['wk'], lw['wvt'], ctab, stab, tm_in)
        r3 = lambda a: a.reshape(b, s, a.shape[-1])
        ol = _lru(r3(xl), r3(gl), lw['cw'], lw['cb'], lw['wa'], lw['ba'], lw['wx'], lw['bx'], lw['lam'],
                  lw['gn_lru'], ts_lru)
        fs = jnp.concatenate([lw['lam0'], far_bias])
        om, od = _attention(r3(qm), r3(qd), r3(km), r3(kd), vmt, vdt, bt, lw['dl'], fs, t_attn)
        f2 = lambda a: a.reshape(t, a.shape[-1])
        x2 = _mid(lw['lam1'], x2d, f2(om), f2(ol), f2(od), lw['gm'], lw['gd'], lw['bd'], lw['wout'], lw['gc'],
                  lw['wcq'], lw['kv'], lw['wco'], tm_mid, s)
        h3, eb, rk, ea, cnt = _score(x2, lw['g_ffn'], lw['wq'], lw['keyext'], tm_score)
        x3 = _peer(h3, x2, lw['u'], lw['vt'], eb, rk, ea, cnt, tm_peer, eblk)
        return x3, None

    x2d, _ = lax.scan(layer, x.reshape(t, d), w)
    return _final_norm(x2d, p['norm_final'][None, :], min(512, t)).reshape(b, s, d)


def kernel(x, mem, positions, rel_bias, norm_mix, w_in, mla_q_norm, mla_w_uq, mla_kv_norm, mla_w_ukv, lru_conv_w, lru_conv_b, lru_w_a, lru_b_a, lru_w_x, lru_b_x, lru_lambda, diff_lambda, diff_norm, out_norm_mla, out_norm_lru, w_out, norm_cross, norm_mem, w_cq, w_ckv, w_co, norm_ffn, peer_w_q, peer_keys, peer_u, peer_v, norm_final):
    p = dict(x=x, mem=mem, positions=positions, rel_bias=rel_bias, norm_mix=norm_mix, w_in=w_in,
             mla_q_norm=mla_q_norm, mla_w_uq=mla_w_uq, mla_kv_norm=mla_kv_norm, mla_w_ukv=mla_w_ukv,
             lru_conv_w=lru_conv_w, lru_conv_b=lru_conv_b, lru_w_a=lru_w_a, lru_b_a=lru_b_a, lru_w_x=lru_w_x,
             lru_b_x=lru_b_x, lru_lambda=lru_lambda, diff_lambda=diff_lambda, diff_norm=diff_norm,
             out_norm_mla=out_norm_mla, out_norm_lru=out_norm_lru, w_out=w_out, norm_cross=norm_cross,
             norm_mem=norm_mem, w_cq=w_cq, w_ckv=w_ckv, w_co=w_co, norm_ffn=norm_ffn, peer_w_q=peer_w_q,
             peer_keys=peer_keys, peer_u=peer_u, peer_v=peer_v, norm_final=norm_final)
    return _forward(p, tm_in=512, ts_lru=256, t_attn=512, tm_mid=512, tm_score=256, tm_peer=512, eblk=1024)
```
